```python
import math, functools
import jax, jax.numpy as jnp
from jax import lax
import numpy as np

D_MODEL = 1024
BATCH = 4
SEQ = 4096
DEPTH = 4
DEC_BATCH = 128
DEC_SEQ = 4
PAST_LEN = 2048
PAGE_SIZE = 128

N_MIXERS = 3
BLK = 128
EPS = 1e-6
NEG = -1e30
A_KV_HEADS = 8
A_HD = 128
A_GROUPS = ((128, 1), (512, 4), (2048, 16))
A_STEPS = 128
A_WIN = 2048
A_Q = len(A_GROUPS) * A_KV_HEADS * A_HD
A_KV = A_KV_HEADS * A_HD
B_HEADS = 16
B_HD = 64
B_W = B_HEADS * B_HD
C_HEADS = 8
C_HD = 64
C_VD = 2 * C_HD
C_W = C_HEADS * C_VD
X_HEADS = 4
X_HD = 128
X_W = X_HEADS * X_HD
MEM_LEN = 256
MIX_W = 1024
BRANCH_W = MIX_W + X_W
A_IN = A_Q + 2 * A_KV + X_W + BRANCH_W
B_IN = 3 * B_W + X_W + BRANCH_W
C_IN = 3 * C_W + X_W + BRANCH_W
NUM_BUCKETS = 32
MAX_EXACT = NUM_BUCKETS // 2
REL_MAX_DIST = 2048
C_BIAS_OFF = len(A_GROUPS) * A_KV_HEADS
REL_HEADS = C_BIAS_OFF + C_HEADS

kernel_name = "hybrid_dilated_stickbreak_diffattn_decoder_step"


def rmsnorm(x, g):
    xf = x.astype(jnp.float32)
    y = xf * lax.rsqrt(jnp.mean(xf * xf, axis=-1, keepdims=True) + EPS)
    return (y * g.astype(jnp.float32)).astype(x.dtype)


def rel_bucket(dist):
    n = jnp.maximum(dist, 0)
    nf = jnp.maximum(n, 1).astype(jnp.float32)
    large = MAX_EXACT + (jnp.log(nf / MAX_EXACT) / math.log(REL_MAX_DIST / MAX_EXACT)
                         * (NUM_BUCKETS - MAX_EXACT)).astype(jnp.int32)
    large = jnp.minimum(large, NUM_BUCKETS - 1)
    return jnp.where(n < MAX_EXACT, n, large)


def a_group_bias(rel_bias, g):
    d = A_GROUPS[g][1]
    b = rel_bias[rel_bucket(d * jnp.arange(A_STEPS + 1)), g * A_KV_HEADS:(g + 1) * A_KV_HEADS]
    return b.T.astype(jnp.float32)


def gather_pages(pool, page_table):
    g = pool[page_table]
    return g.reshape(g.shape[0], g.shape[1] * g.shape[2], *g.shape[3:])


def dilated_window_prompt(q, k, v, d, bias):
    B, S, H, Dh = q.shape
    L = S // d
    nb = -(-L // BLK)
    Lp = nb * BLK

    def strided(t):
        t = jnp.swapaxes(t.reshape(B, L, d, H, Dh), 1, 2)
        t = jnp.pad(t, ((0, 0), (0, 0), (0, Lp - L), (0, 0), (0, 0)))
        return t.reshape(B, d, nb, BLK, H, Dh)

    def with_prev(t):
        prev = jnp.pad(t, ((0, 0), (0, 0), (1, 0), (0, 0), (0, 0), (0, 0)))[:, :, :-1]
        return jnp.concatenate([prev, t], axis=3)

    qb = strided(q)
    kb = with_prev(strided(k))
    vb = with_prev(strided(v))
    s = jnp.einsum('brnqhd,brnkhd->brnhqk', qb, kb).astype(jnp.float32) * A_HD ** -0.5
    kj = jnp.arange(2 * BLK)[None, :] - BLK
    back = jnp.arange(BLK)[:, None] - kj
    band = (back >= 0) & (back <= A_STEPS)
    valid = band[None] & ((jnp.arange(nb)[:, None, None] * BLK + kj[None]) >= 0)
    s = s + bias[:, jnp.clip(back, 0, A_STEPS)][None, None, None]
    s = jnp.where(valid[None, None, :, None], s, NEG)
    lse = jax.nn.logsumexp(s, axis=-1)
    p = jnp.exp(s - lse[..., None])
    o = jnp.einsum('brnhqk,brnkhd->brnqhd', p.astype(v.dtype), vb)

    def unstrided(t):
        t = t.reshape(B, d, Lp, *t.shape[4:])[:, :, :L]
        return jnp.swapaxes(t, 1, 2).reshape(B, S, *t.shape[3:])

    return unstrided(o), unstrided(jnp.swapaxes(lse, 3, 4))


def dilated_window_sample(q, k_all, v_all, d, bias, n_buf):
    DS = q.shape[1]
    idx = n_buf + jnp.arange(DS)[:, None] - d * jnp.arange(A_STEPS + 1)[None, :]
    valid = idx >= 0
    idx = jnp.maximum(idx, 0)
    kg = k_all[:, idx]
    vg = v_all[:, idx]
    s = jnp.einsum('bihd,bikhd->bhik', q, kg).astype(jnp.float32) * A_HD ** -0.5 + bias[:, None, :]
    s = jnp.where(valid[None, None], s, NEG)
    lse = jax.nn.logsumexp(s, axis=-1)
    p = jnp.exp(s - lse[..., None])
    o = jnp.einsum('bhik,bikhd->bihd', p.astype(v_all.dtype), vg)
    return o, jnp.swapaxes(lse, 1, 2)


def combine_groups(outs):
    o = jnp.stack([a for a, _ in outs])
    lse = jnp.stack([b for _, b in outs])
    w = jax.nn.softmax(lse, axis=0)
    return jnp.sum(o * w[..., None].astype(o.dtype), axis=0)


def stick_breaking_attend(q, k, v, qpos, kpos):
    z = jnp.einsum('bqhd,bkhd->bhqk', q, k).astype(jnp.float32) * B_HD ** -0.5
    valid = kpos[None, :] < qpos[:, None]
    log_keep = jnp.where(valid, jax.nn.log_sigmoid(-z), 0.0)
    later = lax.cumsum(log_keep, axis=3, reverse=True) - log_keep
    a = jnp.where(valid, jnp.exp(jax.nn.log_sigmoid(z) + later), 0.0)
    return jnp.einsum('bhqk,bkhd->bqhd', a.astype(v.dtype), v)


def diff_attend(q, k, v, qpos, kpos, bias_table, lam):
    s = jnp.einsum('bqhcd,bkhcd->bchqk', q, k).astype(jnp.float32) * C_HD ** -0.5
    dist = qpos[:, None] - kpos[None, :]
    bias = jnp.moveaxis(bias_table[rel_bucket(dist)], -1, 0).astype(jnp.float32)
    s = jnp.where(dist >= 0, s + bias[None, None], NEG)
    p = jax.nn.softmax(s, axis=-1)
    w = p[:, 0] - lam * p[:, 1]
    return jnp.einsum('bhqk,bkhd->bqhd', w.astype(v.dtype), v)


def sweep_query_blocks(attend, q, k, v):
    B, S = q.shape[:2]
    nq = S // BLK
    qb = jnp.moveaxis(q.reshape(B, nq, BLK, *q.shape[2:]), 1, 0)
    kpos = jnp.arange(S)

    def one(args):
        qi, n = args
        return attend(qi, k, v, n * BLK + jnp.arange(BLK), kpos)

    o = lax.map(one, (qb, jnp.arange(nq)))
    return jnp.moveaxis(o, 0, 1).reshape(B, S, *o.shape[3:])


def cross_attend(q, mk, mv):
    s = jnp.einsum('bqhd,bmhd->bhqm', q, mk).astype(jnp.float32) * X_HD ** -0.5
    p = jax.nn.softmax(s, axis=-1)
    return jnp.einsum('bhqm,bmhd->bqhd', p.astype(mv.dtype), mv)


def finish_branch(z, mix_out, mk, mv, w_out_l):
    Bz, T = z.shape[:2]
    n_mix = z.shape[-1] - X_W - BRANCH_W
    xq = z[..., n_mix:n_mix + X_W].reshape(Bz, T, X_HEADS, X_HD)
    cross = cross_attend(xq, mk, mv).reshape(Bz, T, X_W)
    u = jnp.concatenate([mix_out.reshape(Bz, T, MIX_W), cross], axis=-1) * jax.nn.silu(z[..., n_mix + X_W:])
    return u @ w_out_l


def split_a(z):
    Bz, T = z.shape[:2]
    q = z[..., :A_Q].reshape(Bz, T, len(A_GROUPS), A_KV_HEADS, A_HD)
    k = z[..., A_Q:A_Q + A_KV].reshape(Bz, T, A_KV_HEADS, A_HD)
    v = z[..., A_Q + A_KV:A_Q + 2 * A_KV].reshape(Bz, T, A_KV_HEADS, A_HD)
    return q, k, v


def split_b(z):
    Bz, T = z.shape[:2]
    q = z[..., :B_W].reshape(Bz, T, B_HEADS, B_HD)
    k = z[..., B_W:2 * B_W].reshape(Bz, T, B_HEADS, B_HD)
    v = z[..., 2 * B_W:3 * B_W].reshape(Bz, T, B_HEADS, B_HD)
    return q, k, v


def split_c(z):
    Bz, T = z.shape[:2]
    q = z[..., :C_W].reshape(Bz, T, C_HEADS, 2, C_HD)
    k = z[..., C_W:2 * C_W].reshape(Bz, T, C_HEADS, 2, C_HD)
    v = z[..., 2 * C_W:3 * C_W].reshape(Bz, T, C_HEADS, C_VD)
    return q, k, v


def lambda_init(layer):
    return 0.8 - 0.6 * math.exp(-0.3 * layer)


def setup_inputs(seed: int = 0) -> dict:
    key = jax.random.key(seed)
    ks = jax.random.split(key, 24)
    f = jnp.float32
    n_a, n_b, n_c = (len(range(m, DEPTH, N_MIXERS)) for m in range(N_MIXERS))
    n_pages = PAST_LEN // PAGE_SIZE
    n_used = DEC_BATCH * n_pages
    n_pool = n_used + max(1, n_used // 4)
    win = min(A_WIN, PAST_LEN)

    def nrm(k, shape, s=1.0):
        return s * jax.random.normal(k, shape, f)

    page_table = jax.random.permutation(ks[0], n_pool)[:n_used].reshape(DEC_BATCH, n_pages).astype(jnp.int32)
    return {
        "x_prompt": nrm(ks[1], (BATCH, SEQ, D_MODEL)),
        "x_sample": nrm(ks[2], (DEC_BATCH, DEC_SEQ, D_MODEL)),
        "cache_a_k": nrm(ks[3], (n_a, DEC_BATCH, win, A_KV_HEADS, A_HD)),
        "cache_a_v": nrm(ks[4], (n_a, DEC_BATCH, win, A_KV_HEADS, A_HD)),
        "cache_b_k": nrm(ks[5], (n_b, n_pool, PAGE_SIZE, B_HEADS, B_HD)),
        "cache_b_v": nrm(ks[6], (n_b, n_pool, PAGE_SIZE, B_HEADS, B_HD)),
        "cache_c_k": nrm(ks[7], (n_c, n_pool, PAGE_SIZE, C_HEADS, 2 * C_HD)),
        "cache_c_v": nrm(ks[8], (n_c, n_pool, PAGE_SIZE, C_HEADS, C_VD)),
        "cache_mem_k": nrm(ks[9], (DEPTH, DEC_BATCH, MEM_LEN, X_HEADS, X_HD)),
        "cache_mem_v": nrm(ks[10], (DEPTH, DEC_BATCH, MEM_LEN, X_HEADS, X_HD)),
        "page_table": page_table,
        "mem_prompt": nrm(ks[11], (BATCH, MEM_LEN, D_MODEL)),
        "norm_g": 1.0 + nrm(ks[12], (DEPTH, D_MODEL), 0.02),
        "final_g": 1.0 + nrm(ks[13], (D_MODEL,), 0.02),
        "mem_norm_g": 1.0 + nrm(ks[14], (DEPTH, D_MODEL), 0.02),
        "w_mem_kv": nrm(ks[15], (DEPTH, D_MODEL, 2 * X_W), D_MODEL ** -0.5),
        "rel_bias": nrm(ks[16], (NUM_BUCKETS, REL_HEADS), 0.5),
        "w_in_a": nrm(ks[17], (n_a, D_MODEL, A_IN), D_MODEL ** -0.5),
        "w_in_b": nrm(ks[18], (n_b, D_MODEL, B_IN), D_MODEL ** -0.5),
        "w_in_c": nrm(ks[19], (n_c, D_MODEL, C_IN), D_MODEL ** -0.5),
        "c_lambda": nrm(ks[20], (n_c, 4, C_HD), 0.1),
        "c_subln_g": 1.0 + nrm(ks[21], (n_c, C_VD), 0.02),
        "w_out": nrm(ks[22], (DEPTH, BRANCH_W, D_MODEL), BRANCH_W ** -0.5),
    }


def reference(x_prompt, x_sample, cache_a_k, cache_a_v, cache_b_k, cache_b_v, cache_c_k, cache_c_v,
              cache_mem_k, cache_mem_v, page_table, mem_prompt, norm_g, final_g, mem_norm_g, w_mem_kv,
              rel_bias, w_in_a, w_in_b, w_in_c, c_lambda, c_subln_g, w_out):
    Bp, S, _ = x_prompt.shape
    Bs, DS, _ = x_sample.shape
    P = page_table.shape[1] * PAGE_SIZE
    n_buf = cache_a_k.shape[2]
    win_p = min(A_WIN, S)
    a_bias = [a_group_bias(rel_bias, g) for g in range(len(A_GROUPS))]
    c_bias = rel_bias[:, C_BIAS_OFF:C_BIAS_OFF + C_HEADS]
    qpos_s = P + jnp.arange(DS)
    kpos_s = jnp.arange(P + DS)
    xp, xs = x_prompt, x_sample
    a_k_p, a_v_p, b_k_p, b_v_p, c_k_p, c_v_p, mem_k_p, mem_v_p = [], [], [], [], [], [], [], []
    a_k_s, a_v_s, b_k_s, b_v_s, c_k_s, c_v_s = [], [], [], [], [], []

    for l in range(DEPTH):
        m, j = l % N_MIXERS, l // N_MIXERS
        w_in = (w_in_a, w_in_b, w_in_c)[m][j]
        zp = rmsnorm(xp, norm_g[l]) @ w_in
        zs = rmsnorm(xs, norm_g[l]) @ w_in
        mkv = rmsnorm(mem_prompt, mem_norm_g[l]) @ w_mem_kv[l]
        mk_p = mkv[..., :X_W].reshape(Bp, MEM_LEN, X_HEADS, X_HD)
        mv_p = mkv[..., X_W:].reshape(Bp, MEM_LEN, X_HEADS, X_HD)
        mem_k_p.append(mk_p)
        mem_v_p.append(mv_p)

        if m == 0:
            qp, kp, vp = split_a(zp)
            qs, ks_, vs = split_a(zs)
            op = combine_groups([dilated_window_prompt(qp[:, :, g], kp, vp, d, a_bias[g])
                                 for g, (_, d) in enumerate(A_GROUPS)])
            k_all = jnp.concatenate([cache_a_k[j], ks_], axis=1)
            v_all = jnp.concatenate([cache_a_v[j], vs], axis=1)
            os_ = combine_groups([dilated_window_sample(qs[:, :, g], k_all, v_all, d, a_bias[g], n_buf)
                                  for g, (_, d) in enumerate(A_GROUPS)])
            a_k_p.append(kp[:, S - win_p:])
            a_v_p.append(vp[:, S - win_p:])
            a_k_s.append(ks_)
            a_v_s.append(vs)
        elif m == 1:
            qp, kp, vp = split_b(zp)
            qs, ks_, vs = split_b(zs)
            op = sweep_query_blocks(stick_breaking_attend, qp, kp, vp)
            k_all = jnp.concatenate([gather_pages(cache_b_k[j], page_table), ks_], axis=1)
            v_all = jnp.concatenate([gather_pages(cache_b_v[j], page_table), vs], axis=1)
            os_ = stick_breaking_attend(qs, k_all, v_all, qpos_s, kpos_s)
            b_k_p.append(kp)
            b_v_p.append(vp)
            b_k_s.append(ks_)
            b_v_s.append(vs)
        else:
            lam_l = c_lambda[j].astype(jnp.float32)
            lam0 = lambda_init(l)
            lam = (jnp.exp(jnp.sum(lam_l[0] * lam_l[1])) - jnp.exp(jnp.sum(lam_l[2] * lam_l[3])) + lam0)
            attend = functools.partial(diff_attend, bias_table=c_bias, lam=lam)
            qp, kp, vp = split_c(zp)
            qs, ks_, vs = split_c(zs)
            op = sweep_query_blocks(attend, qp, kp, vp)
            k_past = gather_pages(cache_c_k[j], page_table).reshape(Bs, P, C_HEADS, 2, C_HD)
            k_all = jnp.concatenate([k_past, ks_], axis=1)
            v_all = jnp.concatenate([gather_pages(cache_c_v[j], page_table), vs], axis=1)
            os_ = attend(qs, k_all, v_all, qpos_s, kpos_s)
            op = rmsnorm(op, c_subln_g[j]) * (1.0 - lam0)
            os_ = rmsnorm(os_, c_subln_g[j]) * (1.0 - lam0)
            c_k_p.append(kp.reshape(Bp, S, C_HEADS, 2 * C_HD))
            c_v_p.append(vp)
            c_k_s.append(ks_.reshape(Bs, DS, C_HEADS, 2 * C_HD))
            c_v_s.append(vs)

        xp = xp + finish_branch(zp, op, mk_p, mv_p, w_out[l])
        xs = xs + finish_branch(zs, os_, cache_mem_k[l], cache_mem_v[l], w_out[l])

    y_prompt = rmsnorm(xp, final_g)
    y_sample = rmsnorm(xs, final_g)
    return (y_prompt, y_sample,
            jnp.stack(a_k_p), jnp.stack(a_v_p), jnp.stack(b_k_p), jnp.stack(b_v_p),
            jnp.stack(c_k_p), jnp.stack(c_v_p), jnp.stack(mem_k_p), jnp.stack(mem_v_p),
            jnp.stack(a_k_s), jnp.stack(a_v_s), jnp.stack(b_k_s), jnp.stack(b_v_s),
            jnp.stack(c_k_s), jnp.stack(c_v_s))
```

```python
import functools
import math

import numpy as np
import jax
import jax.numpy as jnp
from jax import lax
from jax.experimental import pallas as pl
from jax.experimental.pallas import tpu as pltpu

F32 = jnp.float32
BF16 = jnp.bfloat16

EPS = 1e-6
NEG = -1e30
LANES = 128
SUBLANES = 8
VMEM_CAP = 60 * 1024 * 1024

N_MIXERS = 3
A_KV_HEADS = 8
A_HD = 128
A_DILATIONS = (1, 4, 16)
A_STEPS = 128
A_Q = len(A_DILATIONS) * A_KV_HEADS * A_HD
A_KV = A_KV_HEADS * A_HD
A_BLOCK = A_STEPS * A_DILATIONS[-1]
B_HEADS = 16
B_HD = 64
B_W = B_HEADS * B_HD
C_HEADS = 8
C_HD = 64
C_VD = 2 * C_HD
C_W = C_HEADS * C_VD
X_HEADS = 4
X_HD = 128
X_W = X_HEADS * X_HD
MEM_LEN = 256
MIX_W = 1024
BRANCH_W = MIX_W + X_W
NUM_BUCKETS = 32
MAX_EXACT = NUM_BUCKETS // 2
REL_MAX_DIST = 2048
C_BIAS_OFF = len(A_DILATIONS) * A_KV_HEADS
ROWS_S = SUBLANES


def _dot(a, b):
    return jnp.dot(a, b, preferred_element_type=F32)


def _dot_nt(a, b):
    return lax.dot_general(a, b, (((1,), (1,)), ((), ())), preferred_element_type=F32)


def _params(n_axes, vmem_bytes):
    return pltpu.CompilerParams(
        dimension_semantics=("arbitrary",) * n_axes,
        vmem_limit_bytes=int(min(VMEM_CAP, max(vmem_bytes, 16 * 1024 * 1024))))


def _rms_scale(x):
    return lax.rsqrt(jnp.mean(x * x, axis=-1, keepdims=True) + EPS)


def _split3(x):
    hi = x.astype(BF16)
    r1 = x - hi.astype(F32)
    mid = r1.astype(BF16)
    lo = (r1 - mid.astype(F32)).astype(BF16)
    return hi, mid, lo


def _softplus(z):
    return jnp.maximum(z, 0.0) + jnp.log(1.0 + jnp.exp(-jnp.abs(z)))


def _norm_matmul_kernel(x_ref, g_ref, w_ref, o_ref, xn_ref):
    @pl.when(pl.program_id(1) == 0)
    def _():
        x = x_ref[...]
        xn_ref[...] = (x * _rms_scale(x) * g_ref[...]).astype(BF16)

    o_ref[...] = _dot(xn_ref[...], w_ref[...])


def _norm_matmul(x, g, w):
    t, d = x.shape
    n = w.shape[1]
    tm = min(t, 1024)
    tn = min(n, 1024)
    assert t % tm == 0 and n % tn == 0
    vmem = 2 * (tm * d * 4 + d * tn * 2 + tm * tn * 4) + tm * d * 2 + (4 << 20)
    return pl.pallas_call(
        _norm_matmul_kernel,
        grid=(t // tm, n // tn),
        in_specs=[pl.BlockSpec((tm, d), lambda i, j: (i, 0)),
                  pl.BlockSpec((1, d), lambda i, j: (0, 0)),
                  pl.BlockSpec((d, tn), lambda i, j: (0, j))],
        out_specs=pl.BlockSpec((tm, tn), lambda i, j: (i, j)),
        out_shape=jax.ShapeDtypeStruct((t, n), F32),
        scratch_shapes=[pltpu.VMEM((tm, d), BF16)],
        compiler_params=_params(2, vmem),
        name="norm_matmul",
    )(x, g.reshape(1, d), w)


def _cross_kernel(q_ref, mk_ref, mv_ref, o_ref, *, nb):
    scale = X_HD ** -0.5
    for n in range(nb):
        outs = []
        for h in range(X_HEADS):
            sl = slice(h * X_HD, (h + 1) * X_HD)
            s = _dot_nt(q_ref[n, :, sl].astype(BF16), mk_ref[n, :, sl].astype(BF16)) * scale
            p = jnp.exp(s - jnp.max(s, axis=-1, keepdims=True))
            l = jnp.sum(p, axis=-1, keepdims=True)
            outs.append(_dot(p.astype(BF16), mv_ref[n, :, sl].astype(BF16)) / l)
        o_ref[n] = jnp.concatenate(outs, axis=-1)


def _cross_attend(z3, q_col, mk, mk_col, mv, mv_col, *, nb, rows):
    g, r, _ = z3.shape
    assert g % nb == 0 and r % rows == 0
    vmem = 2 * nb * (2 * rows * X_W * 4 + 2 * MEM_LEN * X_W * 4) + (8 << 20)
    return pl.pallas_call(
        functools.partial(_cross_kernel, nb=nb),
        grid=(g // nb, r // rows),
        in_specs=[pl.BlockSpec((nb, rows, X_W), lambda i, j: (i, j, q_col)),
                  pl.BlockSpec((nb, MEM_LEN, X_W), lambda i, j: (i, 0, mk_col)),
                  pl.BlockSpec((nb, MEM_LEN, X_W), lambda i, j: (i, 0, mv_col))],
        out_specs=pl.BlockSpec((nb, rows, X_W), lambda i, j: (i, j, 0)),
        out_shape=jax.ShapeDtypeStruct((g, r, X_W), F32),
        compiler_params=_params(2, vmem),
        name="cross_attend",
    )(z3, mk, mv)


def _silu(v):
    return v * (1.0 / (1.0 + jnp.exp(-v)))


def _gate_out_kernel(g0_ref, g1_ref, g2_ref, mix_ref, cross_ref, w_ref, x_ref, fg_ref, o_ref, *, final):
    half = MIX_W // 2
    u0 = (mix_ref[:, :half] * _silu(g0_ref[...])).astype(BF16)
    u1 = (mix_ref[:, half:] * _silu(g1_ref[...])).astype(BF16)
    u2 = (cross_ref[...] * _silu(g2_ref[...])).astype(BF16)
    y = x_ref[...] + (_dot(u0, w_ref[:half]) + _dot(u1, w_ref[half:MIX_W]) + _dot(u2, w_ref[MIX_W:]))
    if final:
        y = y * _rms_scale(y) * fg_ref[...]
    o_ref[...] = y


def _gate_out(z, gate_col, mix, cross, w_out, x, final_g, *, final):
    t, d = x.shape
    tm = min(t, 512)
    assert t % tm == 0 and X_W == MIX_W // 2
    vmem = 2 * tm * 4 * (3 * X_W + MIX_W + X_W + 2 * d) + 2 * BRANCH_W * d * 2 + (8 << 20)
    gate_spec = lambda c: pl.BlockSpec((tm, X_W), lambda i: (i, gate_col + c))
    return pl.pallas_call(
        functools.partial(_gate_out_kernel, final=final),
        grid=(t // tm,),
        in_specs=[gate_spec(0), gate_spec(1), gate_spec(2),
                  pl.BlockSpec((tm, MIX_W), lambda i: (i, 0)),
                  pl.BlockSpec((tm, X_W), lambda i: (i, 0)),
                  pl.BlockSpec((BRANCH_W, d), lambda i: (0, 0)),
                  pl.BlockSpec((tm, d), lambda i: (i, 0)),
                  pl.BlockSpec((1, d), lambda i: (0, 0))],
        out_specs=pl.BlockSpec((tm, d), lambda i: (i, 0)),
        out_shape=jax.ShapeDtypeStruct((t, d), F32),
        compiler_params=_params(1, vmem),
        name="gate_out",
    )(z, z, z, mix, cross, w_out, x, final_g.reshape(1, d))


def _rel_bucket(dist):
    n = jnp.maximum(dist, 0)
    nf = jnp.maximum(n, 1).astype(F32)
    large = MAX_EXACT + (jnp.log(nf / MAX_EXACT) / math.log(REL_MAX_DIST / MAX_EXACT)
                         * (NUM_BUCKETS - MAX_EXACT)).astype(jnp.int32)
    large = jnp.minimum(large, NUM_BUCKETS - 1)
    return jnp.where(n < MAX_EXACT, n, large)


def _a_bias_by_step(rel_bias):
    out = []
    for g, d in enumerate(A_DILATIONS):
        b = rel_bias[_rel_bucket(d * jnp.arange(A_STEPS + 1)), g * A_KV_HEADS:(g + 1) * A_KV_HEADS]
        out.append(b.T.astype(F32))
    return jnp.stack(out)


def _a_prompt_bias_tiles(a_bias):
    back = np.arange(A_STEPS)[:, None] - (np.arange(2 * A_STEPS)[None, :] - A_STEPS)
    band = (back >= 0) & (back <= A_STEPS)
    tiles = a_bias[:, :, np.clip(back, 0, A_STEPS)]
    return jnp.where(band[None, None], tiles, NEG)


def _c_bias_by_dist(rel_bias, n):
    return rel_bias[_rel_bucket(jnp.arange(n)), C_BIAS_OFF:C_BIAS_OFF + C_HEADS].T.astype(F32)


def _c_prompt_bias_tiles(bias_by_dist, s, t):
    nd = s // t
    dist = (np.arange(nd)[:, None, None] * t + np.arange(t)[None, :, None] - np.arange(t)[None, None, :])
    tiles = bias_by_dist[:, np.maximum(dist, 0)]
    return jnp.where((dist >= 0)[None], tiles, NEG)


def _a_prompt_kernel(q0_ref, q1_ref, q2_ref, kc_ref, kp_ref, vc_ref, vp_ref, bias_ref, o_ref,
                     kk_ref, vv_ref, og_ref, lg_ref):
    blk = A_BLOCK
    first_block = pl.program_id(2) == 0
    kk_ref[:blk] = kp_ref[0]
    kk_ref[blk:] = kc_ref[0]
    vv_ref[:blk] = vp_ref[0]
    vv_ref[blk:] = vc_ref[0]
    col = lax.broadcasted_iota(jnp.int32, (A_STEPS, 2 * A_STEPS), 1)
    scale = A_HD ** -0.5

    for g, (d, q_ref) in enumerate(zip(A_DILATIONS, (q0_ref, q1_ref, q2_ref))):
        span = A_STEPS * d
        n_sub = blk // span
        bias = bias_ref[g, 0]

        def unit(u, _, d=d, g=g, q_ref=q_ref, span=span, bias=bias):
            sb = u // d
            r = u - sb * d
            q0 = sb * span + r
            k0 = blk + q0 - span
            if d == 1:
                q = q_ref[0, pl.ds(q0, A_STEPS), :]
                k = kk_ref[pl.ds(k0, 2 * A_STEPS), :]
                v = vv_ref[pl.ds(k0, 2 * A_STEPS), :]
            else:
                q = q_ref[0, pl.ds(q0, A_STEPS, stride=d), :]
                k = kk_ref[pl.ds(k0, 2 * A_STEPS, stride=d), :]
                v = vv_ref[pl.ds(k0, 2 * A_STEPS, stride=d), :]
            s = _dot_nt(q.astype(BF16), k.astype(BF16)) * scale + bias
            n_dead = jnp.where(jnp.logical_and(first_block, sb == 0), A_STEPS, 0)
            s = jnp.where(col < n_dead, NEG, s)
            m = jnp.max(s, axis=-1, keepdims=True)
            p = jnp.exp(s - m)
            l = jnp.sum(p, axis=-1, keepdims=True)
            o = _dot(p.astype(BF16), v.astype(BF16)) / l
            lse = jnp.broadcast_to(m + jnp.log(l), (A_STEPS, A_HD))
            if d == 1:
                og_ref[g, pl.ds(q0, A_STEPS), :] = o
                lg_ref[g, pl.ds(q0, A_STEPS), :] = lse
            else:
                og_ref[g, pl.ds(q0, A_STEPS, stride=d), :] = o
                lg_ref[g, pl.ds(q0, A_STEPS, stride=d), :] = lse
            return 0

        lax.fori_loop(0, n_sub * d, unit, 0)

    l0, l1, l2 = lg_ref[0], lg_ref[1], lg_ref[2]
    m = jnp.maximum(jnp.maximum(l0, l1), l2)
    e0, e1, e2 = jnp.exp(l0 - m), jnp.exp(l1 - m), jnp.exp(l2 - m)
    den = e0 + e1 + e2
    o_ref[0] = (og_ref[0] * (e0 / den) + og_ref[1] * (e1 / den) + og_ref[2] * (e2 / den))


def _a_prompt(z3, bias_tiles):
    b, s, _ = z3.shape
    blk = A_BLOCK
    assert s % blk == 0
    h = A_KV_HEADS
    qspec = lambda g: pl.BlockSpec((1, blk, A_HD), lambda bi, hi, n: (bi, n, g * h + hi))
    kcol, vcol = A_Q // A_HD, (A_Q + A_KV) // A_HD
    cur = lambda c: pl.BlockSpec((1, blk, A_HD), lambda bi, hi, n: (bi, n, c + hi))
    prev = lambda c: pl.BlockSpec((1, blk, A_HD), lambda bi, hi, n: (bi, jnp.maximum(n - 1, 0), c + hi))
    vmem = 2 * 8 * blk * A_HD * 4 + 2 * 3 * A_STEPS * 2 * A_STEPS * 4 + (2 * 2 + 6) * blk * A_HD * 4 + (8 << 20)
    return pl.pallas_call(
        _a_prompt_kernel,
        grid=(b, h, s // blk),
        in_specs=[qspec(0), qspec(1), qspec(2), cur(kcol), prev(kcol), cur(vcol), prev(vcol),
                  pl.BlockSpec((len(A_DILATIONS), 1, A_STEPS, 2 * A_STEPS), lambda bi, hi, n: (0, hi, 0, 0))],
        out_specs=pl.BlockSpec((1, blk, A_HD), lambda bi, hi, n: (bi, n, hi)),
        out_shape=jax.ShapeDtypeStruct((b, s, A_KV), F32),
        scratch_shapes=[pltpu.VMEM((2 * blk, A_HD), F32), pltpu.VMEM((2 * blk, A_HD), F32),
                        pltpu.VMEM((len(A_DILATIONS), blk, A_HD), F32),
                        pltpu.VMEM((len(A_DILATIONS), blk, A_HD), F32)],
        compiler_params=_params(3, vmem),
        name="a_prompt",
    )(z3, z3, z3, z3, z3, z3, z3, bias_tiles)


def _later_matrix(t):
    return jnp.asarray(np.tril(np.ones((t, t), np.float32), -1), BF16)


def _stick_tile(z, later_mat, carry, valid):
    sp = _softplus(z)
    lk = -sp
    if valid is not None:
        lk = jnp.where(valid, lk, 0.0)
    hi, mid, lo = _split3(lk)
    later = _dot(hi, later_mat) + _dot(mid, later_mat) + _dot(lo, later_mat)
    a = jnp.exp(z - sp + later + carry)
    if valid is not None:
        a = jnp.where(valid, a, 0.0)
    return a, carry + jnp.sum(lk, axis=-1, keepdims=True)


def _b_prompt_kernel(q_ref, k_ref, v_ref, t_ref, o_ref, *, tq):
    qi = pl.program_id(2)
    q = q_ref[0] * (B_HD ** -0.5)
    lane = lax.broadcasted_iota(jnp.int32, (tq, 2 * B_HD), 1)
    row = lax.broadcasted_iota(jnp.int32, (tq, tq), 0)
    col = lax.broadcasted_iota(jnp.int32, (tq, tq), 1)
    strictly_earlier = col < row
    later_mat = t_ref[...]
    outs = []
    for hh in range(2):
        qm = jnp.where((lane >= B_HD) == (hh == 1), q, 0.0).astype(BF16)

        def tile(kb, carry, acc, valid, qm=qm):
            start = pl.multiple_of(kb * tq, tq)
            k = k_ref[0, pl.ds(start, tq), :].astype(BF16)
            v = v_ref[0, pl.ds(start, tq), :].astype(BF16)
            a, carry = _stick_tile(_dot_nt(qm, k), later_mat, carry, valid)
            return carry, acc + _dot(a.astype(BF16), v)

        carry, acc = tile(qi, jnp.zeros((tq, 1), F32), jnp.zeros((tq, 2 * B_HD), F32), strictly_earlier)
        carry, acc = lax.fori_loop(0, qi, lambda it, c: tile(qi - 1 - it, c[0], c[1], None), (carry, acc))
        outs.append(acc)
    o_ref[0] = jnp.where(lane < B_HD, outs[0], outs[1])


def _b_prompt(z3):
    b, s, _ = z3.shape
    tq = min(s, 256)
    assert s % tq == 0
    pairs = B_W // LANES
    vmem = 2 * (2 * s * LANES * 4 + 2 * tq * LANES * 4) + 2 * tq * tq * 2 + (12 << 20)
    return pl.pallas_call(
        functools.partial(_b_prompt_kernel, tq=tq),
        grid=(b, pairs, s // tq),
        in_specs=[pl.BlockSpec((1, tq, LANES), lambda bi, hp, qi: (bi, qi, hp)),
                  pl.BlockSpec((1, s, LANES), lambda bi, hp, qi: (bi, 0, pairs + hp)),
                  pl.BlockSpec((1, s, LANES), lambda bi, hp, qi: (bi, 0, 2 * pairs + hp)),
                  pl.BlockSpec((tq, tq), lambda bi, hp, qi: (0, 0))],
        out_specs=pl.BlockSpec((1, tq, LANES), lambda bi, hp, qi: (bi, qi, hp)),
        out_shape=jax.ShapeDtypeStruct((b, s, B_W), F32),
        compiler_params=_params(3, vmem),
        name="b_prompt",
    )(z3, z3, z3, _later_matrix(tq))


def _diff_lambda(lam_ref, lam0):
    t = lam_ref[...]
    a = jnp.sum(t[0:1] * t[1:2], axis=-1, keepdims=True)
    b = jnp.sum(t[2:3] * t[3:4], axis=-1, keepdims=True)
    return jnp.exp(a) - jnp.exp(b) + lam0


def _online_softmax_step(s, v, m, l, acc):
    m_new = jnp.maximum(m, jnp.max(s, axis=-1, keepdims=True))
    alpha = jnp.exp(m - m_new)
    p = jnp.exp(s - m_new)
    return m_new, alpha * l + jnp.sum(p, axis=-1, keepdims=True), alpha * acc + _dot(p.astype(BF16), v)


def _c_prompt_kernel(q_ref, k_ref, v_ref, bias_ref, lam_ref, g_ref, o_ref, *, tq, lam0):
    qi = pl.program_id(2)
    q = q_ref[0] * (C_HD ** -0.5)
    lane = lax.broadcasted_iota(jnp.int32, (tq, C_VD), 1)
    q1 = jnp.where(lane < C_HD, q, 0.0).astype(BF16)
    q2 = jnp.where(lane >= C_HD, q, 0.0).astype(BF16)

    def body(kb, c):
        m1, l1, a1, m2, l2, a2 = c
        start = pl.multiple_of(kb * tq, tq)
        k = k_ref[0, pl.ds(start, tq), :].astype(BF16)
        v = v_ref[0, pl.ds(start, tq), :].astype(BF16)
        bias = bias_ref[0, qi - kb]
        m1, l1, a1 = _online_softmax_step(_dot_nt(q1, k) + bias, v, m1, l1, a1)
        m2, l2, a2 = _online_softmax_step(_dot_nt(q2, k) + bias, v, m2, l2, a2)
        return m1, l1, a1, m2, l2, a2

    col0 = lambda fill: jnp.full((tq, 1), fill, F32)
    acc0 = jnp.zeros((tq, C_VD), F32)
    m1, l1, a1, m2, l2, a2 = lax.fori_loop(
        0, qi + 1, body, (col0(NEG), col0(0.0), acc0, col0(NEG), col0(0.0), acc0))
    o = a1 / l1 - _diff_lambda(lam_ref, lam0) * (a2 / l2)
    o_ref[0] = o * _rms_scale(o) * g_ref[...] * (1.0 - lam0)


def _c_prompt(z3, bias_tiles, c_lam, subln_g, lam0):
    b, s, _ = z3.shape
    tq = bias_tiles.shape[-1]
    nd = s // tq
    h = C_HEADS
    vmem = 2 * (2 * s * C_VD * 4 + 2 * tq * C_VD * 4 + nd * tq * tq * 4) + (12 << 20)
    return pl.pallas_call(
        functools.partial(_c_prompt_kernel, tq=tq, lam0=lam0),
        grid=(b, h, nd),
        in_specs=[pl.BlockSpec((1, tq, C_VD), lambda bi, hi, qi: (bi, qi, hi)),
                  pl.BlockSpec((1, s, C_VD), lambda bi, hi, qi: (bi, 0, h + hi)),
                  pl.BlockSpec((1, s, C_VD), lambda bi, hi, qi: (bi, 0, 2 * h + hi)),
                  pl.BlockSpec((1, nd, tq, tq), lambda bi, hi, qi: (hi, 0, 0, 0)),
                  pl.BlockSpec((4, C_HD), lambda bi, hi, qi: (0, 0)),
                  pl.BlockSpec((1, C_VD), lambda bi, hi, qi: (0, 0))],
        out_specs=pl.BlockSpec((1, tq, C_VD), lambda bi, hi, qi: (bi, qi, hi)),
        out_shape=jax.ShapeDtypeStruct((b, s, C_W), F32),
        compiler_params=_params(3, vmem),
        name="c_prompt",
    )(z3, z3, z3, bias_tiles, c_lam, subln_g.reshape(1, C_VD))


A_TAIL = 512
A_PICK = 8


def _a_sample_slots(n_buf):
    period = A_DILATIONS[-1]
    tail = np.arange(n_buf - A_TAIL, n_buf)
    picks = (np.arange((n_buf - A_TAIL) // period)[:, None] * period + np.arange(A_PICK)[None, :]).reshape(-1)
    new = np.where(np.arange(LANES) < ROWS_S, n_buf + np.arange(LANES), -1)
    return np.concatenate([tail, picks, new])


def _a_sample_bias(a_bias, n_buf, ds):
    pos = _a_sample_slots(n_buf)
    rows = []
    for g, d in enumerate(A_DILATIONS):
        for i in range(ROWS_S):
            delta = n_buf + i - pos
            ok = (pos >= 0) & (delta >= 0) & (delta % d == 0) & (delta // d <= A_STEPS) & (i < ds)
            ok = ok | ((i >= ds) & (pos == n_buf + i))
            step = np.clip(delta // d, 0, A_STEPS)
            rows.append(jnp.where(ok[None], a_bias[g][:, step], NEG))
    return jnp.stack(rows, axis=1)


def _a_sample_kernel(q_ref, kn_ref, vn_ref, kt_ref, ks_ref, vt_ref, vs_ref, bias_ref, o_ref,
                     knp_ref, vnp_ref, *, n_tail, n_pick):
    scale = A_HD ** -0.5
    n_groups = len(A_DILATIONS)
    knp_ref[...] = jnp.zeros(knp_ref.shape, F32)
    vnp_ref[...] = jnp.zeros(vnp_ref.shape, F32)
    tail_rows = A_TAIL // n_tail
    pick_m = LANES // A_PICK
    for h in range(A_KV_HEADS):
        sl = slice(h * A_HD, (h + 1) * A_HD)
        q = jnp.concatenate([q_ref[0, :, (g * A_KV_HEADS + h) * A_HD:(g * A_KV_HEADS + h + 1) * A_HD]
                             for g in range(n_groups)], axis=0).astype(BF16)
        knp_ref[:ROWS_S] = kn_ref[0, :, sl]
        vnp_ref[:ROWS_S] = vn_ref[0, :, sl]

        def slab(ref, i):
            if i < n_tail:
                m0 = i * tail_rows // 16
                x = ref[0][0, 0, m0:m0 + tail_rows // 16, pl.ds(h, 16, stride=SUBLANES), :]
            elif i < n_tail + n_pick:
                m0 = (i - n_tail) * pick_m
                x = ref[1][0, 0, m0:m0 + pick_m, pl.ds(h, A_PICK, stride=SUBLANES), :]
            else:
                return ref[2][...].astype(BF16)
            return x.reshape(LANES, A_HD).astype(BF16)

        n_slabs = n_tail + n_pick + 1
        krefs, vrefs = (kt_ref, ks_ref, knp_ref), (vt_ref, vs_ref, vnp_ref)
        s = jnp.concatenate([_dot_nt(q, slab(krefs, i)) for i in range(n_slabs)], axis=-1)
        s = s * scale + bias_ref[h]
        m = jnp.max(s, axis=-1, keepdims=True)
        p = jnp.exp(s - m)
        l = jnp.sum(p, axis=-1, keepdims=True)
        pb = p.astype(BF16)
        o = _dot(pb[:, :LANES], slab(vrefs, 0))
        for i in range(1, n_slabs):
            o = o + _dot(pb[:, i * LANES:(i + 1) * LANES], slab(vrefs, i))
        o = o / l
        lse = m + jnp.log(l)
        rows = lambda x, g: x[g * ROWS_S:(g + 1) * ROWS_S]
        mx = jnp.maximum(jnp.maximum(rows(lse, 0), rows(lse, 1)), rows(lse, 2))
        e = [jnp.exp(rows(lse, g) - mx) for g in range(n_groups)]
        den = e[0] + e[1] + e[2]
        o_ref[0, :, sl] = sum(rows(o, g) * (e[g] / den) for g in range(n_groups))


def _a_sample(zs3, cache_k, cache_v, j, bias):
    bs = zs3.shape[0]
    n_buf = cache_k.shape[2]
    period = A_DILATIONS[-1]
    assert n_buf % LANES == 0 and n_buf >= A_TAIL and (n_buf - A_TAIL) % (period * LANES // A_PICK) == 0
    n_per = n_buf // period
    tail_per = A_TAIL // period
    n_tail = A_TAIL // LANES
    n_pick = (n_per - tail_per) * A_PICK // LANES
    rows_per = period * A_KV_HEADS
    view = lambda c: c.reshape(c.shape[0], bs, n_per, rows_per, A_HD)
    tail = pl.BlockSpec((1, 1, tail_per, rows_per, A_HD), lambda b: (j, b, n_per // tail_per - 1, 0, 0))
    pick = pl.BlockSpec((1, 1, n_per - tail_per, A_PICK * A_KV_HEADS, A_HD), lambda b: (j, b, 0, 0, 0))
    n_slots = (n_tail + n_pick + 1) * LANES
    n_rows = len(A_DILATIONS) * ROWS_S
    vmem = 2 * 2 * (A_TAIL + (n_per - tail_per) * A_PICK) * A_KV_HEADS * A_HD * 4 + (12 << 20)
    kcol, vcol = A_Q // A_KV, A_Q // A_KV + 1
    return pl.pallas_call(
        functools.partial(_a_sample_kernel, n_tail=n_tail, n_pick=n_pick),
        grid=(bs,),
        in_specs=[pl.BlockSpec((1, ROWS_S, A_Q), lambda b: (b, 0, 0)),
                  pl.BlockSpec((1, ROWS_S, A_KV), lambda b: (b, 0, kcol)),
                  pl.BlockSpec((1, ROWS_S, A_KV), lambda b: (b, 0, vcol)),
                  tail, pick, tail, pick,
                  pl.BlockSpec((A_KV_HEADS, n_rows, n_slots), lambda b: (0, 0, 0))],
        out_specs=pl.BlockSpec((1, ROWS_S, A_KV), lambda b: (b, 0, 0)),
        out_shape=jax.ShapeDtypeStruct((bs, ROWS_S, A_KV), F32),
        scratch_shapes=[pltpu.VMEM((LANES, A_HD), F32), pltpu.VMEM((LANES, A_HD), F32)],
        compiler_params=_params(1, vmem),
        name="a_sample",
    )(zs3, zs3, zs3, view(cache_k), view(cache_k), view(cache_v), view(cache_v), bias)


def _b_sample_kernel(pt_ref, q_ref, kn_ref, vn_ref, kp_ref, vp_ref, t_ref, o_ref,
                     knp_ref, vnp_ref, carry_ref, acc_ref, *, page):
    del pt_ref
    step = pl.program_id(1)
    later_mat = t_ref[...]
    n_rows = B_HEADS * ROWS_S

    def q_head(h):
        return q_ref[0, :, h * B_HD:(h + 1) * B_HD] * (B_HD ** -0.5)

    def keys_of(ref, h):
        return ref[pl.ds(h, page, stride=B_HEADS), :]

    def attend(k_ref, v_ref, valid):
        z = jnp.concatenate([_dot_nt(q_head(h), keys_of(k_ref, h)) for h in range(B_HEADS)], axis=0)
        a, carry = _stick_tile(z, later_mat, carry_ref[...], valid)
        carry_ref[...] = carry
        for h in range(B_HEADS):
            rs = slice(h * ROWS_S, (h + 1) * ROWS_S)
            acc_ref[rs, :] += _dot(a[rs], keys_of(v_ref, h))

    @pl.when(step == 0)
    def _():
        carry_ref[...] = jnp.zeros(carry_ref.shape, F32)
        acc_ref[...] = jnp.zeros(acc_ref.shape, F32)
        knp_ref[...] = jnp.zeros(knp_ref.shape, F32)
        vnp_ref[...] = jnp.zeros(vnp_ref.shape, F32)
        knp_ref[:ROWS_S * B_HEADS] = kn_ref[0]
        vnp_ref[:ROWS_S * B_HEADS] = vn_ref[0]
        i8 = lax.broadcasted_iota(jnp.int32, (n_rows, page), 0) % ROWS_S
        jj = lax.broadcasted_iota(jnp.int32, (n_rows, page), 1)
        attend(knp_ref, vnp_ref, jj < i8)

    attend(kp_ref.at[0, 0], vp_ref.at[0, 0], None)

    @pl.when(step == pl.num_programs(1) - 1)
    def _():
        o_ref[0] = jnp.concatenate([acc_ref[h * ROWS_S:(h + 1) * ROWS_S, :] for h in range(B_HEADS)], axis=-1)


def _b_sample(zs3, k_new, v_new, cache_k, cache_v, j, page_table):
    bs = zs3.shape[0]
    n_pages = page_table.shape[1]
    page = cache_k.shape[2]
    assert page == LANES
    rows = page * B_HEADS
    view = lambda c: c.reshape(c.shape[0], c.shape[1], rows, B_HD)
    page_spec = pl.BlockSpec((1, 1, rows, B_HD), lambda b, p, pt: (j, pt[b, n_pages - 1 - p], 0, 0))
    new_spec = pl.BlockSpec((1, ROWS_S * B_HEADS, B_HD), lambda b, p, pt: (b, 0, 0))
    vmem = 2 * 2 * rows * LANES * 4 + 2 * rows * LANES * 4 + (8 << 20)
    grid_spec = pltpu.PrefetchScalarGridSpec(
        num_scalar_prefetch=1,
        grid=(bs, n_pages),
        in_specs=[pl.BlockSpec((1, ROWS_S, B_W), lambda b, p, pt: (b, 0, 0)),
                  new_spec, new_spec, page_spec, page_spec,
                  pl.BlockSpec((page, page), lambda b, p, pt: (0, 0))],
        out_specs=pl.BlockSpec((1, ROWS_S, B_W), lambda b, p, pt: (b, 0, 0)),
        scratch_shapes=[pltpu.VMEM((rows, B_HD), F32), pltpu.VMEM((rows, B_HD), F32),
                        pltpu.VMEM((B_HEADS * ROWS_S, 1), F32), pltpu.VMEM((B_HEADS * ROWS_S, B_HD), F32)])
    return pl.pallas_call(
        functools.partial(_b_sample_kernel, page=page),
        grid_spec=grid_spec,
        out_shape=jax.ShapeDtypeStruct((bs, ROWS_S, B_W), F32),
        compiler_params=_params(2, vmem),
        name="b_sample",
    )(page_table, zs3, k_new, v_new, view(cache_k), view(cache_v), _later_matrix(page))


def _c_sample_bias(bias_by_dist, n_pages, page, ds):
    past = n_pages * page
    i = np.arange(ROWS_S)
    dist_past = past + i[None, :, None] - (np.arange(n_pages)[:, None, None] * page + np.arange(page)[None, None, :])
    dist_new = i[:, None] - np.arange(page)[None, :]
    ok_new = (dist_new >= 0) & (np.arange(page)[None, :] < max(ds, 1))
    tiles_past = bias_by_dist[:, dist_past]
    tiles_new = jnp.where(ok_new[None], bias_by_dist[:, np.maximum(dist_new, 0)], NEG)[:, None]
    tiles = jnp.concatenate([tiles_past, tiles_new], axis=1)
    tiles = jnp.broadcast_to(tiles[:, :, None], (C_HEADS, n_pages + 1, 2, ROWS_S, page))
    return jnp.moveaxis(tiles, 1, 0).reshape(n_pages + 1, C_HEADS * 2 * ROWS_S, page)


def _c_sample_kernel(pt_ref, q_ref, kn_ref, vn_ref, kp_ref, vp_ref, bnew_ref, bias_ref, lam_ref, g_ref, o_ref,
                     knp_ref, vnp_ref, m_ref, l_ref, acc_ref, *, page, lam0):
    del pt_ref
    step = pl.program_id(1)
    rows_h = 2 * ROWS_S
    lane = lax.broadcasted_iota(jnp.int32, (ROWS_S, C_VD), 1)

    def q_head(h):
        q = q_ref[0, :, h * C_VD:(h + 1) * C_VD] * (C_HD ** -0.5)
        return jnp.concatenate([jnp.where(lane < C_HD, q, 0.0), jnp.where(lane >= C_HD, q, 0.0)],
                               axis=0).astype(BF16)

    def keys_of(ref, h):
        return ref[pl.ds(h, page, stride=C_HEADS), :].astype(BF16)

    def attend(k_ref, v_ref, bias):
        s = jnp.concatenate([_dot_nt(q_head(h), keys_of(k_ref, h)) for h in range(C_HEADS)], axis=0) + bias
        m_old = m_ref[...]
        m_new = jnp.maximum(m_old, jnp.max(s, axis=-1, keepdims=True))
        alpha = jnp.exp(m_old - m_new)
        p = jnp.exp(s - m_new)
        m_ref[...] = m_new
        l_ref[...] = alpha * l_ref[...] + jnp.sum(p, axis=-1, keepdims=True)
        pb = p.astype(BF16)
        for h in range(C_HEADS):
            rs = slice(h * rows_h, (h + 1) * rows_h)
            acc_ref[rs, :] = alpha[rs] * acc_ref[rs, :] + _dot(pb[rs], keys_of(v_ref, h))

    @pl.when(step == 0)
    def _():
        m_ref[...] = jnp.full(m_ref.shape, NEG, F32)
        l_ref[...] = jnp.zeros(l_ref.shape, F32)
        acc_ref[...] = jnp.zeros(acc_ref.shape, F32)
        knp_ref[...] = jnp.zeros(knp_ref.shape, F32)
        vnp_ref[...] = jnp.zeros(vnp_ref.shape, F32)
        knp_ref[:ROWS_S * C_HEADS] = kn_ref[0]
        vnp_ref[:ROWS_S * C_HEADS] = vn_ref[0]
        attend(knp_ref, vnp_ref, bnew_ref[0])

    attend(kp_ref.at[0, 0], vp_ref.at[0, 0], bias_ref[0])

    @pl.when(step == pl.num_programs(1) - 1)
    def _():
        lam = _diff_lambda(lam_ref, lam0)
        norm = acc_ref[...] / l_ref[...]
        outs = []
        for h in range(C_HEADS):
            o = norm[h * rows_h:h * rows_h + ROWS_S] - lam * norm[h * rows_h + ROWS_S:(h + 1) * rows_h]
            outs.append(o * _rms_scale(o) * g_ref[...] * (1.0 - lam0))
        o_ref[0] = jnp.concatenate(outs, axis=-1)


def _c_sample(zs3, k_new, v_new, cache_k, cache_v, j, page_table, bias, c_lam, subln_g, lam0):
    bs = zs3.shape[0]
    n_pages = page_table.shape[1]
    page = cache_k.shape[2]
    assert page == LANES
    rows = page * C_HEADS
    n_rows = C_HEADS * 2 * ROWS_S
    view = lambda c: c.reshape(c.shape[0], c.shape[1], rows, C_VD)
    page_spec = pl.BlockSpec((1, 1, rows, C_VD), lambda b, p, pt: (j, pt[b, p], 0, 0))
    new_spec = pl.BlockSpec((1, ROWS_S * C_HEADS, C_VD), lambda b, p, pt: (b, 0, 0))
    vmem = 2 * 2 * rows * C_VD * 4 + 2 * rows * C_VD * 4 + (8 << 20)
    grid_spec = pltpu.PrefetchScalarGridSpec(
        num_scalar_prefetch=1,
        grid=(bs, n_pages),
        in_specs=[pl.BlockSpec((1, ROWS_S, C_W), lambda b, p, pt: (b, 0, 0)),
                  new_spec, new_spec, page_spec, page_spec,
                  pl.BlockSpec((1, n_rows, page), lambda b, p, pt: (n_pages, 0, 0)),
                  pl.BlockSpec((1, n_rows, page), lambda b, p, pt: (p, 0, 0)),
                  pl.BlockSpec((4, C_HD), lambda b, p, pt: (0, 0)),
                  pl.BlockSpec((1, C_VD), lambda b, p, pt: (0, 0))],
        out_specs=pl.BlockSpec((1, ROWS_S, C_W), lambda b, p, pt: (b, 0, 0)),
        scratch_shapes=[pltpu.VMEM((rows, C_VD), F32), pltpu.VMEM((rows, C_VD), F32),
                        pltpu.VMEM((n_rows, 1), F32), pltpu.VMEM((n_rows, 1), F32),
                        pltpu.VMEM((n_rows, C_VD), F32)])
    return pl.pallas_call(
        functools.partial(_c_sample_kernel, page=page, lam0=lam0),
        grid_spec=grid_spec,
        out_shape=jax.ShapeDtypeStruct((bs, ROWS_S, C_W), F32),
        compiler_params=_params(2, vmem),
        name="c_sample",
    )(page_table, zs3, k_new, v_new, view(cache_k), view(cache_v), bias, bias, c_lam, subln_g.reshape(1, C_VD))


def _lambda_init(layer):
    return 0.8 - 0.6 * math.exp(-0.3 * layer)


def kernel(x_prompt, x_sample, cache_a_k, cache_a_v, cache_b_k, cache_b_v, cache_c_k, cache_c_v,
           cache_mem_k, cache_mem_v, page_table, mem_prompt, norm_g, final_g, mem_norm_g, w_mem_kv,
           rel_bias, w_in_a, w_in_b, w_in_c, c_lambda, c_subln_g, w_out):
    bp, s, d = x_prompt.shape
    bs, ds, _ = x_sample.shape
    depth = norm_g.shape[0]
    n_pages, page = page_table.shape[1], cache_b_k.shape[2]
    n_buf = cache_a_k.shape[2]
    win_p = min(n_buf, s)
    assert ds <= ROWS_S

    a_bias = _a_bias_by_step(rel_bias)
    a_tiles = _a_prompt_bias_tiles(a_bias)
    a_sample_bias = _a_sample_bias(a_bias, n_buf, ds)
    c_tile = min(s, 256)
    c_by_dist = _c_bias_by_dist(rel_bias, max(s, n_pages * page + ROWS_S))
    c_tiles = _c_prompt_bias_tiles(c_by_dist, s, c_tile)
    c_sample_bias = _c_sample_bias(c_by_dist, n_pages, page, ds)

    xp = x_prompt.reshape(bp * s, d)
    xs = jnp.pad(x_sample, ((0, 0), (0, ROWS_S - ds), (0, 0))).reshape(bs * ROWS_S, d)
    mem = mem_prompt.reshape(bp * MEM_LEN, d)
    outs = {name: [] for name in ("a_k_p", "a_v_p", "b_k_p", "b_v_p", "c_k_p", "c_v_p", "mem_k_p", "mem_v_p",
                                  "a_k_s", "a_v_s", "b_k_s", "b_v_s", "c_k_s", "c_v_s")}

    for l in range(depth):
        m, j = l % N_MIXERS, l // N_MIXERS
        w_in = (w_in_a, w_in_b, w_in_c)[m][j].astype(BF16)
        n_in = w_in.shape[1]
        n_mix = n_in - X_W - BRANCH_W
        zp = _norm_matmul(xp, norm_g[l], w_in)
        zs = _norm_matmul(xs, norm_g[l], w_in)
        mkv = _norm_matmul(mem, mem_norm_g[l], w_mem_kv[l].astype(BF16)).reshape(bp, MEM_LEN, 2 * X_W)
        outs["mem_k_p"].append(mkv[..., :X_W].reshape(bp, MEM_LEN, X_HEADS, X_HD))
        outs["mem_v_p"].append(mkv[..., X_W:].reshape(bp, MEM_LEN, X_HEADS, X_HD))
        zp3 = zp.reshape(bp, s, n_in)
        zs3 = zs.reshape(bs, ROWS_S, n_in)

        if m == 0:
            op = _a_prompt(zp3, a_tiles)
            os_ = _a_sample(zs3, cache_a_k, cache_a_v, j, a_sample_bias)
            heads = (A_KV_HEADS, A_HD)
            kcols, vcols = slice(A_Q, A_Q + A_KV), slice(A_Q + A_KV, A_Q + 2 * A_KV)
            outs["a_k_p"].append(zp3[:, s - win_p:, kcols].reshape(bp, win_p, *heads))
            outs["a_v_p"].append(zp3[:, s - win_p:, vcols].reshape(bp, win_p, *heads))
            outs["a_k_s"].append(zs3[:, :ds, kcols].reshape(bs, ds, *heads))
            outs["a_v_s"].append(zs3[:, :ds, vcols].reshape(bs, ds, *heads))
        elif m == 1:
            heads = (B_HEADS, B_HD)
            kcols, vcols = slice(B_W, 2 * B_W), slice(2 * B_W, 3 * B_W)
            k_new = zs3[:, :, kcols].reshape(bs, ROWS_S * B_HEADS, B_HD)
            v_new = zs3[:, :, vcols].reshape(bs, ROWS_S * B_HEADS, B_HD)
            op = _b_prompt(zp3)
            os_ = _b_sample(zs3, k_new, v_new, cache_b_k, cache_b_v, j, page_table)
            outs["b_k_p"].append(zp3[:, :, kcols].reshape(bp, s, *heads))
            outs["b_v_p"].append(zp3[:, :, vcols].reshape(bp, s, *heads))
            outs["b_k_s"].append(zs3[:, :ds, kcols].reshape(bs, ds, *heads))
            outs["b_v_s"].append(zs3[:, :ds, vcols].reshape(bs, ds, *heads))
        else:
            heads = (C_HEADS, C_VD)
            kcols, vcols = slice(C_W, 2 * C_W), slice(2 * C_W, 3 * C_W)
            lam0 = _lambda_init(l)
            k_new = zs3[:, :, kcols].reshape(bs, ROWS_S * C_HEADS, C_VD)
            v_new = zs3[:, :, vcols].reshape(bs, ROWS_S * C_HEADS, C_VD)
            op = _c_prompt(zp3, c_tiles, c_lambda[j], c_subln_g[j], lam0)
            os_ = _c_sample(zs3, k_new, v_new, cache_c_k, cache_c_v, j, page_table, c_sample_bias,
                            c_lambda[j], c_subln_g[j], lam0)
            outs["c_k_p"].append(zp3[:, :, kcols].reshape(bp, s, *heads))
            outs["c_v_p"].append(zp3[:, :, vcols].reshape(bp, s, *heads))
            outs["c_k_s"].append(zs3[:, :ds, kcols].reshape(bs, ds, *heads))
            outs["c_v_s"].append(zs3[:, :ds, vcols].reshape(bs, ds, *heads))

        q_col = n_mix // X_W
        cross_p = _cross_attend(zp3, q_col, mkv, 0, mkv, 1, nb=1, rows=min(s, 512))
        mem_k = cache_mem_k[l].reshape(bs, MEM_LEN, X_W)
        mem_v = cache_mem_v[l].reshape(bs, MEM_LEN, X_W)
        cross_s = _cross_attend(zs3, q_col, mem_k, 0, mem_v, 0, nb=8, rows=ROWS_S)
        last = l == depth - 1
        xp = _gate_out(zp, q_col + 1, op.reshape(bp * s, MIX_W), cross_p.reshape(bp * s, X_W),
                       w_out[l].astype(BF16), xp, final_g, final=last)
        xs = _gate_out(zs, q_col + 1, os_.reshape(bs * ROWS_S, MIX_W), cross_s.reshape(bs * ROWS_S, X_W),
                       w_out[l].astype(BF16), xs, final_g, final=last)

    y_prompt = xp.reshape(bp, s, d)
    y_sample = xs.reshape(bs, ROWS_S, d)[:, :ds]
    stack = lambda name: jnp.stack(outs[name])
    return (y_prompt, y_sample,
            stack("a_k_p"), stack("a_v_p"), stack("b_k_p"), stack("b_v_p"), stack("c_k_p"), stack("c_v_p"),
            stack("mem_k_p"), stack("mem_v_p"),
            stack("a_k_s"), stack("a_v_s"), stack("b_k_s"), stack("b_v_s"), stack("c_k_s"), stack("c_v_s"))
```

```python
import functools
import math

import numpy as np
import jax
import jax.numpy as jnp
from jax import lax
from jax.experimental import pallas as pl
from jax.experimental.pallas import tpu as pltpu

F32 = jnp.float32
BF16 = jnp.bfloat16

EPS = 1e-6
NEG = -1e30
LANES = 128
SUBLANES = 8
VMEM_CAP = 60 * 1024 * 1024

N_MIXERS = 3
A_KV_HEADS = 8
A_HD = 128
A_DILATIONS = (1, 4, 16)
A_STEPS = 128
A_Q = len(A_DILATIONS) * A_KV_HEADS * A_HD
A_KV = A_KV_HEADS * A_HD
A_BLOCK = A_STEPS * A_DILATIONS[-1]
B_HEADS = 16
B_HD = 64
B_W = B_HEADS * B_HD
C_HEADS = 8
C_HD = 64
C_VD = 2 * C_HD
C_W = C_HEADS * C_VD
X_HEADS = 4
X_HD = 128
X_W = X_HEADS * X_HD
MEM_LEN = 256
MIX_W = 1024
BRANCH_W = MIX_W + X_W
NUM_BUCKETS = 32
MAX_EXACT = NUM_BUCKETS // 2
REL_MAX_DIST = 2048
C_BIAS_OFF = len(A_DILATIONS) * A_KV_HEADS
ROWS_S = SUBLANES


def _dot(a, b):
    return jnp.dot(a, b, preferred_element_type=F32)


def _dot_nt(a, b):
    return lax.dot_general(a, b, (((1,), (1,)), ((), ())), preferred_element_type=F32)


def _params(n_axes, vmem_bytes):
    return pltpu.CompilerParams(
        dimension_semantics=("arbitrary",) * n_axes,
        vmem_limit_bytes=int(min(VMEM_CAP, max(vmem_bytes, 16 * 1024 * 1024))))


def _rms_scale(x):
    return lax.rsqrt(jnp.mean(x * x, axis=-1, keepdims=True) + EPS)


def _split2(x):
    hi = x.astype(BF16)
    return hi, (x - hi.astype(F32)).astype(BF16)


def _softplus(z):
    return jnp.maximum(z, 0.0) + jnp.log(1.0 + jnp.exp(-jnp.abs(z)))


def _norm_matmul_kernel(x_ref, g_ref, w_ref, o_ref, xn_ref):
    @pl.when(pl.program_id(1) == 0)
    def _():
        x = x_ref[...]
        xn_ref[...] = (x * _rms_scale(x) * g_ref[...]).astype(BF16)

    o_ref[...] = _dot(xn_ref[...], w_ref[...])


def _norm_matmul(x, g, w):
    t, d = x.shape
    n = w.shape[1]
    tm = min(t, 1024)
    tn = min(n, 1024)
    assert t % tm == 0 and n % tn == 0
    vmem = 2 * (tm * d * 4 + d * tn * 2 + tm * tn * 4) + tm * d * 2 + (4 << 20)
    return pl.pallas_call(
        _norm_matmul_kernel,
        grid=(t // tm, n // tn),
        in_specs=[pl.BlockSpec((tm, d), lambda i, j: (i, 0)),
                  pl.BlockSpec((1, d), lambda i, j: (0, 0)),
                  pl.BlockSpec((d, tn), lambda i, j: (0, j))],
        out_specs=pl.BlockSpec((tm, tn), lambda i, j: (i, j)),
        out_shape=jax.ShapeDtypeStruct((t, n), F32),
        scratch_shapes=[pltpu.VMEM((tm, d), BF16)],
        compiler_params=_params(2, vmem),
        name="norm_matmul",
    )(x, g.reshape(1, d), w)


def _cross_kernel(q_ref, mk_ref, mv_ref, o_ref, *, nb):
    scale = X_HD ** -0.5
    for n in range(nb):
        outs = []
        for h in range(X_HEADS):
            sl = slice(h * X_HD, (h + 1) * X_HD)
            s = _dot_nt(q_ref[n, :, sl].astype(BF16), mk_ref[0, n, :, sl].astype(BF16)) * scale
            p = jnp.exp(s - jnp.max(s, axis=-1, keepdims=True))
            l = jnp.sum(p, axis=-1, keepdims=True)
            outs.append(_dot(p.astype(BF16), mv_ref[0, n, :, sl].astype(BF16)) / l)
        o_ref[n] = jnp.concatenate(outs, axis=-1)


def _cross_attend(z3, q_col, mk, mk_col, mv, mv_col, layer, *, nb, rows):
    g, r, _ = z3.shape
    assert g % nb == 0 and r % rows == 0
    vmem = 2 * nb * (2 * rows * X_W * 4 + 2 * MEM_LEN * X_W * 4) + (8 << 20)
    return pl.pallas_call(
        functools.partial(_cross_kernel, nb=nb),
        grid=(g // nb, r // rows),
        in_specs=[pl.BlockSpec((nb, rows, X_W), lambda i, j: (i, j, q_col)),
                  pl.BlockSpec((1, nb, MEM_LEN, X_W), lambda i, j: (layer, i, 0, mk_col)),
                  pl.BlockSpec((1, nb, MEM_LEN, X_W), lambda i, j: (layer, i, 0, mv_col))],
        out_specs=pl.BlockSpec((nb, rows, X_W), lambda i, j: (i, j, 0)),
        out_shape=jax.ShapeDtypeStruct((g, r, X_W), F32),
        compiler_params=_params(2, vmem),
        name="cross_attend",
    )(z3, mk, mv)


def _silu(v):
    return v * (1.0 / (1.0 + jnp.exp(-v)))


def _gate_out_kernel(g0_ref, g1_ref, g2_ref, mix_ref, cross_ref, w_ref, x_ref, fg_ref, o_ref, *, final):
    half = MIX_W // 2
    u0 = (mix_ref[:, :half] * _silu(g0_ref[...])).astype(BF16)
    u1 = (mix_ref[:, half:] * _silu(g1_ref[...])).astype(BF16)
    u2 = (cross_ref[...] * _silu(g2_ref[...])).astype(BF16)
    y = x_ref[...] + (_dot(u0, w_ref[:half]) + _dot(u1, w_ref[half:MIX_W]) + _dot(u2, w_ref[MIX_W:]))
    if final:
        y = y * _rms_scale(y) * fg_ref[...]
    o_ref[...] = y


def _gate_out(z, gate_col, mix, cross, w_out, x, final_g, *, final):
    t, d = x.shape
    tm = min(t, 512)
    assert t % tm == 0 and X_W == MIX_W // 2
    vmem = 2 * tm * 4 * (3 * X_W + MIX_W + X_W + 2 * d) + 2 * BRANCH_W * d * 2 + (8 << 20)
    gate_spec = lambda c: pl.BlockSpec((tm, X_W), lambda i: (i, gate_col + c))
    return pl.pallas_call(
        functools.partial(_gate_out_kernel, final=final),
        grid=(t // tm,),
        in_specs=[gate_spec(0), gate_spec(1), gate_spec(2),
                  pl.BlockSpec((tm, MIX_W), lambda i: (i, 0)),
                  pl.BlockSpec((tm, X_W), lambda i: (i, 0)),
                  pl.BlockSpec((BRANCH_W, d), lambda i: (0, 0)),
                  pl.BlockSpec((tm, d), lambda i: (i, 0)),
                  pl.BlockSpec((1, d), lambda i: (0, 0))],
        out_specs=pl.BlockSpec((tm, d), lambda i: (i, 0)),
        out_shape=jax.ShapeDtypeStruct((t, d), F32),
        compiler_params=_params(1, vmem),
        name="gate_out",
    )(z, z, z, mix, cross, w_out, x, final_g.reshape(1, d))


def _rel_bucket(dist):
    n = jnp.maximum(dist, 0)
    nf = jnp.maximum(n, 1).astype(F32)
    large = MAX_EXACT + (jnp.log(nf / MAX_EXACT) / math.log(REL_MAX_DIST / MAX_EXACT)
                         * (NUM_BUCKETS - MAX_EXACT)).astype(jnp.int32)
    large = jnp.minimum(large, NUM_BUCKETS - 1)
    return jnp.where(n < MAX_EXACT, n, large)


def _bias_lookup(cols, dist, valid):
    bucket = _rel_bucket(jnp.asarray(dist, jnp.int32))[None]
    expand = (slice(None),) + (None,) * dist.ndim
    vals = jnp.zeros((cols.shape[1],) + dist.shape, F32)
    for b in range(NUM_BUCKETS):
        vals = jnp.where(bucket == b, cols[b].astype(F32)[expand], vals)
    return jnp.where(jnp.asarray(valid)[None], vals, NEG)


def _a_prompt_bias_tiles(rel_bias):
    back = np.arange(A_STEPS)[:, None] - (np.arange(2 * A_STEPS)[None, :] - A_STEPS)
    band = (back >= 0) & (back <= A_STEPS)
    return jnp.stack([_bias_lookup(rel_bias[:, g * A_KV_HEADS:(g + 1) * A_KV_HEADS],
                                   d * np.clip(back, 0, A_STEPS), band)
                      for g, d in enumerate(A_DILATIONS)])


def _c_prompt_bias_tiles(rel_bias, s, t):
    nd = s // t
    tile = jnp.arange(nd + 1, dtype=jnp.int32)[:, None, None]
    dist = tile * t + jnp.arange(t, dtype=jnp.int32)[None, :, None] - jnp.arange(t, dtype=jnp.int32)[None, None, :]
    dist = jnp.where(tile == nd, -1, dist)
    return _bias_lookup(rel_bias[:, C_BIAS_OFF:C_BIAS_OFF + C_HEADS], jnp.maximum(dist, 0), dist >= 0)


def _a_prompt_kernel(q0_ref, q1_ref, q2_ref, kc_ref, kp_ref, vc_ref, vp_ref, bias_ref, o_ref,
                     kk_ref, vv_ref, og_ref, lg_ref):
    blk = A_BLOCK
    first_block = pl.program_id(2) == 0
    kk_ref[:blk] = kp_ref[0]
    kk_ref[blk:] = kc_ref[0]
    vv_ref[:blk] = vp_ref[0]
    vv_ref[blk:] = vc_ref[0]
    col = lax.broadcasted_iota(jnp.int32, (A_STEPS, 2 * A_STEPS), 1)
    scale = A_HD ** -0.5

    def unit(u, g, d, q_ref):
        span = A_STEPS * d
        sb = u // d
        r = u - sb * d
        q0 = sb * span + r
        k0 = blk + q0 - span
        if d == 1:
            qs, ks = pl.ds(q0, A_STEPS), pl.ds(k0, 2 * A_STEPS)
        else:
            qs, ks = pl.ds(q0, A_STEPS, stride=d), pl.ds(k0, 2 * A_STEPS, stride=d)
        s = _dot_nt(q_ref[0, qs, :].astype(BF16), kk_ref[ks, :].astype(BF16)) * scale + bias_ref[g, 0]
        n_dead = jnp.where(jnp.logical_and(first_block, sb == 0), A_STEPS, 0)
        s = jnp.where(col < n_dead, NEG, s)
        m = jnp.max(s, axis=-1, keepdims=True)
        p = jnp.exp(s - m)
        l = jnp.sum(p, axis=-1, keepdims=True)
        og_ref[g, qs, :] = _dot(p.astype(BF16), vv_ref[ks, :].astype(BF16)) / l
        lg_ref[g, qs, :] = jnp.broadcast_to(m + jnp.log(l), (A_STEPS, A_HD))

    def units(u, _):
        for g, (d, q_ref) in enumerate(zip(A_DILATIONS, (q0_ref, q1_ref, q2_ref))):
            unit(u, g, d, q_ref)
        return 0

    lax.fori_loop(0, blk // A_STEPS, units, 0)

    l0, l1, l2 = lg_ref[0], lg_ref[1], lg_ref[2]
    m = jnp.maximum(jnp.maximum(l0, l1), l2)
    e0, e1, e2 = jnp.exp(l0 - m), jnp.exp(l1 - m), jnp.exp(l2 - m)
    den = e0 + e1 + e2
    o_ref[0] = (og_ref[0] * (e0 / den) + og_ref[1] * (e1 / den) + og_ref[2] * (e2 / den))


def _a_prompt(z3, bias_tiles):
    b, s, _ = z3.shape
    blk = A_BLOCK
    assert s % blk == 0
    h = A_KV_HEADS
    qspec = lambda g: pl.BlockSpec((1, blk, A_HD), lambda bi, hi, n: (bi, n, g * h + hi))
    kcol, vcol = A_Q // A_HD, (A_Q + A_KV) // A_HD
    cur = lambda c: pl.BlockSpec((1, blk, A_HD), lambda bi, hi, n: (bi, n, c + hi))
    prev = lambda c: pl.BlockSpec((1, blk, A_HD), lambda bi, hi, n: (bi, jnp.maximum(n - 1, 0), c + hi))
    vmem = 2 * 8 * blk * A_HD * 4 + 2 * 3 * A_STEPS * 2 * A_STEPS * 4 + (2 * 2 + 6) * blk * A_HD * 4 + (8 << 20)
    return pl.pallas_call(
        _a_prompt_kernel,
        grid=(b, h, s // blk),
        in_specs=[qspec(0), qspec(1), qspec(2), cur(kcol), prev(kcol), cur(vcol), prev(vcol),
                  pl.BlockSpec((len(A_DILATIONS), 1, A_STEPS, 2 * A_STEPS), lambda bi, hi, n: (0, hi, 0, 0))],
        out_specs=pl.BlockSpec((1, blk, A_HD), lambda bi, hi, n: (bi, n, hi)),
        out_shape=jax.ShapeDtypeStruct((b, s, A_KV), F32),
        scratch_shapes=[pltpu.VMEM((2 * blk, A_HD), F32), pltpu.VMEM((2 * blk, A_HD), F32),
                        pltpu.VMEM((len(A_DILATIONS), blk, A_HD), F32),
                        pltpu.VMEM((len(A_DILATIONS), blk, A_HD), F32)],
        compiler_params=_params(3, vmem),
        name="a_prompt",
    )(z3, z3, z3, z3, z3, z3, z3, bias_tiles)


def _later_matrix(t):
    return jnp.asarray(np.tril(np.ones((t, t), np.float32), -1), BF16)


def _stick_tile(z, later_mat, carry, valid):
    t = later_mat.shape[0]
    sp = _softplus(z)
    lk = -sp
    if valid is not None:
        lk = jnp.where(valid, lk, 0.0)
    later = []
    for c in reversed(range(z.shape[1] // t)):
        hi, lo = _split2(lk[:, c * t:(c + 1) * t])
        later.append(_dot(hi, later_mat) + _dot(lo, later_mat) + carry)
        carry = carry + jnp.sum(lk[:, c * t:(c + 1) * t], axis=-1, keepdims=True)
    a = jnp.exp(z - sp + jnp.concatenate(later[::-1], axis=1))
    if valid is not None:
        a = jnp.where(valid, a, 0.0)
    return a, carry


def _b_prompt_kernel(q_ref, k_ref, v_ref, t_ref, o_ref, *, tq):
    qi = pl.program_id(2)
    tk = 2 * tq
    q = q_ref[0] * (B_HD ** -0.5)
    lane = lax.broadcasted_iota(jnp.int32, (tq, 2 * B_HD), 1)
    q2 = jnp.concatenate([jnp.where(lane < B_HD, q, 0.0), jnp.where(lane >= B_HD, q, 0.0)], axis=0).astype(BF16)
    later_mat = t_ref[...]

    def tile(kb, carry, acc, valid):
        start = pl.multiple_of(kb * tk, tk)
        k = k_ref[0, pl.ds(start, tk), :].astype(BF16)
        v = v_ref[0, pl.ds(start, tk), :].astype(BF16)
        a, carry = _stick_tile(_dot_nt(q2, k), later_mat, carry, valid)
        return carry, acc + _dot(a.astype(BF16), v)

    top = qi // 2
    row = lax.broadcasted_iota(jnp.int32, (2 * tq, tk), 0)
    col = lax.broadcasted_iota(jnp.int32, (2 * tq, tk), 1)
    strictly_earlier = col + top * tk < jnp.where(row >= tq, row - tq, row) + qi * tq
    carry, acc = tile(top, jnp.zeros((2 * tq, 1), F32), jnp.zeros((2 * tq, 2 * B_HD), F32), strictly_earlier)
    carry, acc = lax.fori_loop(0, top, lambda it, c: tile(top - 1 - it, c[0], c[1], None), (carry, acc))
    o_ref[0] = jnp.where(lane < B_HD, acc[:tq], acc[tq:])


def _b_prompt(z3):
    b, s, _ = z3.shape
    tq = min(s // 2, 256)
    assert s % (2 * tq) == 0
    pairs = B_W // LANES
    vmem = 2 * (2 * s * LANES * 4 + 2 * tq * LANES * 4) + 2 * tq * tq * 2 + (12 << 20)
    return pl.pallas_call(
        functools.partial(_b_prompt_kernel, tq=tq),
        grid=(b, pairs, s // tq),
        in_specs=[pl.BlockSpec((1, tq, LANES), lambda bi, hp, qi: (bi, qi, hp)),
                  pl.BlockSpec((1, s, LANES), lambda bi, hp, qi: (bi, 0, pairs + hp)),
                  pl.BlockSpec((1, s, LANES), lambda bi, hp, qi: (bi, 0, 2 * pairs + hp)),
                  pl.BlockSpec((tq, tq), lambda bi, hp, qi: (0, 0))],
        out_specs=pl.BlockSpec((1, tq, LANES), lambda bi, hp, qi: (bi, qi, hp)),
        out_shape=jax.ShapeDtypeStruct((b, s, B_W), F32),
        compiler_params=_params(3, vmem),
        name="b_prompt",
    )(z3, z3, z3, _later_matrix(tq))


def _diff_lambda(lam_ref, lam0):
    t = lam_ref[...]
    a = jnp.sum(t[0:1] * t[1:2], axis=-1, keepdims=True)
    b = jnp.sum(t[2:3] * t[3:4], axis=-1, keepdims=True)
    return jnp.exp(a) - jnp.exp(b) + lam0


def _online_softmax_step(s, v, m, l, acc):
    m_new = jnp.maximum(m, jnp.max(s, axis=-1, keepdims=True))
    alpha = jnp.exp(m - m_new)
    p = jnp.exp(s - m_new)
    return m_new, alpha * l + jnp.sum(p, axis=-1, keepdims=True), alpha * acc + _dot(p.astype(BF16), v)


def _c_prompt_kernel(q_ref, k_ref, v_ref, bias_ref, lam_ref, g_ref, o_ref, *, tq, lam0):
    qi = pl.program_id(2)
    n_tiles = bias_ref.shape[1]
    q = q_ref[0] * (C_HD ** -0.5)
    lane = lax.broadcasted_iota(jnp.int32, (tq, C_VD), 1)
    q2 = jnp.concatenate([jnp.where(lane < C_HD, q, 0.0), jnp.where(lane >= C_HD, q, 0.0)], axis=0).astype(BF16)

    def body(kb2, c):
        start = pl.multiple_of(kb2 * 2 * tq, 2 * tq)
        k = k_ref[0, pl.ds(start, 2 * tq), :].astype(BF16)
        v = v_ref[0, pl.ds(start, 2 * tq), :].astype(BF16)
        nd0 = qi - 2 * kb2
        nd1 = jnp.where(nd0 == 0, n_tiles - 1, nd0 - 1)
        bias = jnp.concatenate([bias_ref[0, nd0], bias_ref[0, nd1]], axis=1)
        s = _dot_nt(q2, k) + jnp.concatenate([bias, bias], axis=0)
        return _online_softmax_step(s, v, *c)

    col0 = lambda fill: jnp.full((2 * tq, 1), fill, F32)
    m, l, acc = lax.fori_loop(0, qi // 2 + 1, body, (col0(NEG), col0(0.0), jnp.zeros((2 * tq, C_VD), F32)))
    norm = acc / l
    o = norm[:tq] - _diff_lambda(lam_ref, lam0) * norm[tq:]
    o_ref[0] = o * _rms_scale(o) * g_ref[...] * (1.0 - lam0)


def _c_prompt(z3, bias_tiles, c_lam, subln_g, lam0):
    b, s, _ = z3.shape
    tq = bias_tiles.shape[-1]
    nd = s // tq
    assert s % (2 * tq) == 0 and bias_tiles.shape[1] == nd + 1
    h = C_HEADS
    vmem = 2 * (2 * s * C_VD * 4 + 2 * tq * C_VD * 4 + (nd + 1) * tq * tq * 4) + (16 << 20)
    return pl.pallas_call(
        functools.partial(_c_prompt_kernel, tq=tq, lam0=lam0),
        grid=(b, h, nd),
        in_specs=[pl.BlockSpec((1, tq, C_VD), lambda bi, hi, qi: (bi, qi, hi)),
                  pl.BlockSpec((1, s, C_VD), lambda bi, hi, qi: (bi, 0, h + hi)),
                  pl.BlockSpec((1, s, C_VD), lambda bi, hi, qi: (bi, 0, 2 * h + hi)),
                  pl.BlockSpec((1, nd + 1, tq, tq), lambda bi, hi, qi: (hi, 0, 0, 0)),
                  pl.BlockSpec((4, C_HD), lambda bi, hi, qi: (0, 0)),
                  pl.BlockSpec((1, C_VD), lambda bi, hi, qi: (0, 0))],
        out_specs=pl.BlockSpec((1, tq, C_VD), lambda bi, hi, qi: (bi, qi, hi)),
        out_shape=jax.ShapeDtypeStruct((b, s, C_W), F32),
        compiler_params=_params(3, vmem),
        name="c_prompt",
    )(z3, z3, z3, bias_tiles, c_lam, subln_g.reshape(1, C_VD))


A_TAIL = 512
A_PICK = 8


def _a_sample_slots(n_buf):
    period = A_DILATIONS[-1]
    tail = np.arange(n_buf - A_TAIL, n_buf)
    picks = (np.arange((n_buf - A_TAIL) // period)[:, None] * period + np.arange(A_PICK)[None, :]).reshape(-1)
    new = np.where(np.arange(LANES) < ROWS_S, n_buf + np.arange(LANES), -1)
    return np.concatenate([tail, picks, new])


def _a_sample_bias(rel_bias, n_buf, ds):
    pos = _a_sample_slots(n_buf)
    i = np.arange(ROWS_S)[:, None]
    delta = n_buf + i - pos[None, :]
    groups = []
    for g, d in enumerate(A_DILATIONS):
        ok = (pos[None] >= 0) & (delta >= 0) & (delta % d == 0) & (delta // d <= A_STEPS) & (i < ds)
        ok = ok | ((i >= ds) & (pos[None] == n_buf + i))
        groups.append(_bias_lookup(rel_bias[:, g * A_KV_HEADS:(g + 1) * A_KV_HEADS],
                                   np.clip(delta, 0, d * A_STEPS), ok))
    return jnp.concatenate(groups, axis=1)


def _a_sample_kernel(q_ref, kn_ref, vn_ref, kt_ref, ks_ref, vt_ref, vs_ref, bias_ref, o_ref,
                     knp_ref, vnp_ref, *, n_tail, n_pick):
    scale = A_HD ** -0.5
    n_groups = len(A_DILATIONS)
    knp_ref[...] = jnp.zeros(knp_ref.shape, F32)
    vnp_ref[...] = jnp.zeros(vnp_ref.shape, F32)
    tail_rows = A_TAIL // n_tail
    pick_m = LANES // A_PICK
    for h in range(A_KV_HEADS):
        sl = slice(h * A_HD, (h + 1) * A_HD)
        q = jnp.concatenate([q_ref[0, :, (g * A_KV_HEADS + h) * A_HD:(g * A_KV_HEADS + h + 1) * A_HD]
                             for g in range(n_groups)], axis=0).astype(BF16)
        knp_ref[:ROWS_S] = kn_ref[0, :, sl]
        vnp_ref[:ROWS_S] = vn_ref[0, :, sl]

        def slab(ref, i):
            if i < n_tail:
                m0 = i * tail_rows // 16
                x = ref[0][0, 0, m0:m0 + tail_rows // 16, pl.ds(h, 16, stride=SUBLANES), :]
            elif i < n_tail + n_pick:
                m0 = (i - n_tail) * pick_m
                x = ref[1][0, 0, m0:m0 + pick_m, pl.ds(h, A_PICK, stride=SUBLANES), :]
            else:
                return ref[2][...].astype(BF16)
            return x.reshape(LANES, A_HD).astype(BF16)

        n_slabs = n_tail + n_pick + 1
        krefs, vrefs = (kt_ref, ks_ref, knp_ref), (vt_ref, vs_ref, vnp_ref)
        s = jnp.concatenate([_dot_nt(q, slab(krefs, i)) for i in range(n_slabs)], axis=-1)
        s = s * scale + bias_ref[h]
        m = jnp.max(s, axis=-1, keepdims=True)
        p = jnp.exp(s - m)
        l = jnp.sum(p, axis=-1, keepdims=True)
        pb = p.astype(BF16)
        o = _dot(pb[:, :LANES], slab(vrefs, 0))
        for i in range(1, n_slabs):
            o = o + _dot(pb[:, i * LANES:(i + 1) * LANES], slab(vrefs, i))
        o = o / l
        lse = m + jnp.log(l)
        rows = lambda x, g: x[g * ROWS_S:(g + 1) * ROWS_S]
        mx = jnp.maximum(jnp.maximum(rows(lse, 0), rows(lse, 1)), rows(lse, 2))
        e = [jnp.exp(rows(lse, g) - mx) for g in range(n_groups)]
        den = e[0] + e[1] + e[2]
        o_ref[0, :, sl] = sum(rows(o, g) * (e[g] / den) for g in range(n_groups))


def _a_sample(zs3, cache_k, cache_v, j, bias):
    bs = zs3.shape[0]
    n_buf = cache_k.shape[2]
    period = A_DILATIONS[-1]
    assert n_buf % LANES == 0 and n_buf >= A_TAIL and (n_buf - A_TAIL) % (period * LANES // A_PICK) == 0
    n_per = n_buf // period
    tail_per = A_TAIL // period
    n_tail = A_TAIL // LANES
    n_pick = (n_per - tail_per) * A_PICK // LANES
    rows_per = period * A_KV_HEADS
    view = lambda c: c.reshape(c.shape[0], bs, n_per, rows_per, A_HD)
    tail = pl.BlockSpec((1, 1, tail_per, rows_per, A_HD), lambda b: (j, b, n_per // tail_per - 1, 0, 0))
    pick = pl.BlockSpec((1, 1, n_per - tail_per, A_PICK * A_KV_HEADS, A_HD), lambda b: (j, b, 0, 0, 0))
    n_slots = (n_tail + n_pick + 1) * LANES
    n_rows = len(A_DILATIONS) * ROWS_S
    vmem = 2 * 2 * (A_TAIL + (n_per - tail_per) * A_PICK) * A_KV_HEADS * A_HD * 4 + (12 << 20)
    kcol, vcol = A_Q // A_KV, A_Q // A_KV + 1
    return pl.pallas_call(
        functools.partial(_a_sample_kernel, n_tail=n_tail, n_pick=n_pick),
        grid=(bs,),
        in_specs=[pl.BlockSpec((1, ROWS_S, A_Q), lambda b: (b, 0, 0)),
                  pl.BlockSpec((1, ROWS_S, A_KV), lambda b: (b, 0, kcol)),
                  pl.BlockSpec((1, ROWS_S, A_KV), lambda b: (b, 0, vcol)),
                  tail, pick, tail, pick,
                  pl.BlockSpec((A_KV_HEADS, n_rows, n_slots), lambda b: (0, 0, 0))],
        out_specs=pl.BlockSpec((1, ROWS_S, A_KV), lambda b: (b, 0, 0)),
        out_shape=jax.ShapeDtypeStruct((bs, ROWS_S, A_KV), F32),
        scratch_shapes=[pltpu.VMEM((LANES, A_HD), F32), pltpu.VMEM((LANES, A_HD), F32)],
        compiler_params=_params(1, vmem),
        name="a_sample",
    )(zs3, zs3, zs3, view(cache_k), view(cache_k), view(cache_v), view(cache_v), bias)


PAGES_PER_STEP = 8


def _b_sample_kernel(pt_ref, q_ref, kn_ref, vn_ref, *refs, page, n_sub):
    del pt_ref
    kp_refs, vp_refs = refs[:n_sub], refs[n_sub:2 * n_sub]
    t_ref, o_ref, qbd_ref, knp_ref, vnp_ref, carry_ref, acc_ref = refs[2 * n_sub:]
    step = pl.program_id(1)
    later_mat = t_ref[...]
    n_rows = B_HEADS * ROWS_S
    own_head = (lax.broadcasted_iota(jnp.int32, (n_rows, B_W), 0) // ROWS_S
                == lax.broadcasted_iota(jnp.int32, (n_rows, B_W), 1) // B_HD)

    def attend(z, v_nt, valid):
        a, carry = _stick_tile(z, later_mat, carry_ref[...], valid)
        carry_ref[...] = carry
        ab = a.astype(BF16)
        acc_ref[...] += _dot(ab, vnp_ref[...].astype(BF16)) if v_nt is None else _dot_nt(ab, v_nt)

    @pl.when(step == 0)
    def _():
        carry_ref[...] = jnp.zeros(carry_ref.shape, F32)
        acc_ref[...] = jnp.zeros(acc_ref.shape, F32)
        knp_ref[...] = jnp.zeros(knp_ref.shape, F32)
        vnp_ref[...] = jnp.zeros(vnp_ref.shape, F32)
        knp_ref[:ROWS_S] = kn_ref[0]
        vnp_ref[:ROWS_S] = vn_ref[0]
        q = q_ref[0] * (B_HD ** -0.5)
        qbd = jnp.where(own_head, jnp.concatenate([q] * B_HEADS, axis=0), 0.0).astype(BF16)
        qbd_ref[...] = qbd
        i8 = lax.broadcasted_iota(jnp.int32, (n_rows, page), 0) % ROWS_S
        jj = lax.broadcasted_iota(jnp.int32, (n_rows, page), 1)
        attend(_dot_nt(qbd, knp_ref[...].astype(BF16)), None, jj < i8)

    nt = lambda page_refs: jnp.concatenate([r[0, 0].reshape(B_W, page).astype(BF16) for r in page_refs], axis=1)
    attend(_dot(qbd_ref[...], nt(kp_refs)), nt(vp_refs), None)

    @pl.when(step == pl.num_programs(1) - 1)
    def _():
        own = jnp.where(own_head, acc_ref[...], 0.0).reshape(B_HEADS, ROWS_S, B_W)
        o_ref[0] = jnp.sum(own, axis=0)


def _b_sample(zs3, cache_k, cache_v, j, page_table):
    bs = zs3.shape[0]
    n_pages = page_table.shape[1]
    page = cache_k.shape[2]
    n_sub = PAGES_PER_STEP
    assert page == LANES and n_pages % n_sub == 0
    view = lambda c: jnp.transpose(c, (0, 1, 3, 4, 2))
    page_spec = lambda u: pl.BlockSpec(
        (1, 1, B_HEADS, B_HD, page), lambda b, s, pt: (j, pt[b, n_pages - n_sub * (s + 1) + u], 0, 0, 0))
    kcol, vcol = 1, 2
    n_rows = B_HEADS * ROWS_S
    vmem = 2 * 2 * n_sub * B_W * page * 4 + 4 * page * B_W * 4 + (12 << 20)
    grid_spec = pltpu.PrefetchScalarGridSpec(
        num_scalar_prefetch=1,
        grid=(bs, n_pages // n_sub),
        in_specs=[pl.BlockSpec((1, ROWS_S, B_W), lambda b, s, pt: (b, 0, 0)),
                  pl.BlockSpec((1, ROWS_S, B_W), lambda b, s, pt: (b, 0, kcol)),
                  pl.BlockSpec((1, ROWS_S, B_W), lambda b, s, pt: (b, 0, vcol))]
                 + [page_spec(u) for u in range(n_sub)] * 2
                 + [pl.BlockSpec((page, page), lambda b, s, pt: (0, 0))],
        out_specs=pl.BlockSpec((1, ROWS_S, B_W), lambda b, s, pt: (b, 0, 0)),
        scratch_shapes=[pltpu.VMEM((n_rows, B_W), BF16),
                        pltpu.VMEM((page, B_W), F32), pltpu.VMEM((page, B_W), F32),
                        pltpu.VMEM((n_rows, 1), F32), pltpu.VMEM((n_rows, B_W), F32)])
    kt, vt = view(cache_k), view(cache_v)
    return pl.pallas_call(
        functools.partial(_b_sample_kernel, page=page, n_sub=n_sub),
        grid_spec=grid_spec,
        out_shape=jax.ShapeDtypeStruct((bs, ROWS_S, B_W), F32),
        compiler_params=_params(2, vmem),
        name="b_sample",
    )(page_table, zs3, zs3, zs3, *([kt] * n_sub), *([vt] * n_sub), _later_matrix(page))


def _c_sample_bias(rel_bias, n_pages, page, ds):
    past = n_pages * page
    i = np.arange(ROWS_S)
    dist_past = past + i[None, :, None] - (np.arange(n_pages)[:, None, None] * page + np.arange(page)[None, None, :])
    dist_new = i[:, None] - np.arange(page)[None, :]
    ok_new = (dist_new >= 0) & (np.arange(page)[None, :] < max(ds, 1))
    dist = np.concatenate([dist_past, np.maximum(dist_new, 0)[None]])
    ok = np.concatenate([np.ones_like(dist_past, bool), ok_new[None]])
    tiles = _bias_lookup(rel_bias[:, C_BIAS_OFF:C_BIAS_OFF + C_HEADS], dist, ok)
    tiles = jnp.broadcast_to(tiles[:, :, None], (C_HEADS, n_pages + 1, 2, ROWS_S, page))
    return jnp.moveaxis(tiles, 1, 0).reshape(n_pages + 1, C_HEADS * 2 * ROWS_S, page)


def _c_sample_kernel(pt_ref, q_ref, kn_ref, vn_ref, *refs, page, lam0, n_sub):
    del pt_ref
    kp_refs, vp_refs = refs[:n_sub], refs[n_sub:2 * n_sub]
    bnew_ref, bias_ref, lam_ref, g_ref, o_ref, knp_ref, vnp_ref, m_ref, l_ref, acc_ref = refs[2 * n_sub:]
    step = pl.program_id(1)
    rows_h = 2 * ROWS_S
    lane = lax.broadcasted_iota(jnp.int32, (ROWS_S, C_VD), 1)

    def q_head(h):
        q = q_ref[0, :, h * C_VD:(h + 1) * C_VD] * (C_HD ** -0.5)
        return jnp.concatenate([jnp.where(lane < C_HD, q, 0.0), jnp.where(lane >= C_HD, q, 0.0)],
                               axis=0).astype(BF16)

    def paged(page_refs):
        return lambda h: jnp.concatenate(
            [r[0, 0, pl.ds(h, page, stride=C_HEADS), :] for r in page_refs], axis=0).astype(BF16)

    def fresh(ref):
        return lambda h: ref[:, h * C_VD:(h + 1) * C_VD].astype(BF16)

    def attend(keys_of, vals_of, bias):
        s = jnp.concatenate([_dot_nt(q_head(h), keys_of(h)) for h in range(C_HEADS)], axis=0) + bias
        m_old = m_ref[...]
        m_new = jnp.maximum(m_old, jnp.max(s, axis=-1, keepdims=True))
        alpha = jnp.exp(m_old - m_new)
        p = jnp.exp(s - m_new)
        m_ref[...] = m_new
        l_ref[...] = alpha * l_ref[...] + jnp.sum(p, axis=-1, keepdims=True)
        pb = p.astype(BF16)
        for h in range(C_HEADS):
            rs = slice(h * rows_h, (h + 1) * rows_h)
            acc_ref[rs, :] = alpha[rs] * acc_ref[rs, :] + _dot(pb[rs], vals_of(h))

    @pl.when(step == 0)
    def _():
        m_ref[...] = jnp.full(m_ref.shape, NEG, F32)
        l_ref[...] = jnp.zeros(l_ref.shape, F32)
        acc_ref[...] = jnp.zeros(acc_ref.shape, F32)
        knp_ref[...] = jnp.zeros(knp_ref.shape, F32)
        vnp_ref[...] = jnp.zeros(vnp_ref.shape, F32)
        knp_ref[:ROWS_S] = kn_ref[0]
        vnp_ref[:ROWS_S] = vn_ref[0]
        attend(fresh(knp_ref), fresh(vnp_ref), bnew_ref[0])

    attend(paged(kp_refs), paged(vp_refs), jnp.concatenate([bias_ref[u] for u in range(n_sub)], axis=1))

    @pl.when(step == pl.num_programs(1) - 1)
    def _():
        lam = _diff_lambda(lam_ref, lam0)
        norm = acc_ref[...] / l_ref[...]
        outs = []
        for h in range(C_HEADS):
            o = norm[h * rows_h:h * rows_h + ROWS_S] - lam * norm[h * rows_h + ROWS_S:(h + 1) * rows_h]
            outs.append(o * _rms_scale(o) * g_ref[...] * (1.0 - lam0))
        o_ref[0] = jnp.concatenate(outs, axis=-1)


def _c_sample(zs3, cache_k, cache_v, j, page_table, bias, c_lam, subln_g, lam0):
    bs = zs3.shape[0]
    n_pages = page_table.shape[1]
    page = cache_k.shape[2]
    n_sub = PAGES_PER_STEP
    assert page == LANES and n_pages % n_sub == 0
    rows = page * C_HEADS
    n_rows = C_HEADS * 2 * ROWS_S
    view = lambda c: c.reshape(c.shape[0], c.shape[1], rows, C_VD)
    page_spec = lambda u: pl.BlockSpec((1, 1, rows, C_VD), lambda b, s, pt: (j, pt[b, n_sub * s + u], 0, 0))
    kcol, vcol = 1, 2
    vmem = 2 * 2 * n_sub * rows * C_VD * 4 + 2 * page * C_W * 4 + (12 << 20)
    grid_spec = pltpu.PrefetchScalarGridSpec(
        num_scalar_prefetch=1,
        grid=(bs, n_pages // n_sub),
        in_specs=[pl.BlockSpec((1, ROWS_S, C_W), lambda b, s, pt: (b, 0, 0)),
                  pl.BlockSpec((1, ROWS_S, C_W), lambda b, s, pt: (b, 0, kcol)),
                  pl.BlockSpec((1, ROWS_S, C_W), lambda b, s, pt: (b, 0, vcol))]
                 + [page_spec(u) for u in range(n_sub)] * 2
                 + [pl.BlockSpec((1, n_rows, page), lambda b, s, pt: (n_pages, 0, 0)),
                    pl.BlockSpec((n_sub, n_rows, page), lambda b, s, pt: (s, 0, 0)),
                    pl.BlockSpec((4, C_HD), lambda b, s, pt: (0, 0)),
                    pl.BlockSpec((1, C_VD), lambda b, s, pt: (0, 0))],
        out_specs=pl.BlockSpec((1, ROWS_S, C_W), lambda b, s, pt: (b, 0, 0)),
        scratch_shapes=[pltpu.VMEM((page, C_W), F32), pltpu.VMEM((page, C_W), F32),
                        pltpu.VMEM((n_rows, 1), F32), pltpu.VMEM((n_rows, 1), F32),
                        pltpu.VMEM((n_rows, C_VD), F32)])
    kc, vc = view(cache_k), view(cache_v)
    return pl.pallas_call(
        functools.partial(_c_sample_kernel, page=page, lam0=lam0, n_sub=n_sub),
        grid_spec=grid_spec,
        out_shape=jax.ShapeDtypeStruct((bs, ROWS_S, C_W), F32),
        compiler_params=_params(2, vmem),
        name="c_sample",
    )(page_table, zs3, zs3, zs3, *([kc] * n_sub), *([vc] * n_sub), bias, bias, c_lam, subln_g.reshape(1, C_VD))


def _lambda_init(layer):
    return 0.8 - 0.6 * math.exp(-0.3 * layer)


def kernel(x_prompt, x_sample, cache_a_k, cache_a_v, cache_b_k, cache_b_v, cache_c_k, cache_c_v,
           cache_mem_k, cache_mem_v, page_table, mem_prompt, norm_g, final_g, mem_norm_g, w_mem_kv,
           rel_bias, w_in_a, w_in_b, w_in_c, c_lambda, c_subln_g, w_out):
    bp, s, d = x_prompt.shape
    bs, ds, _ = x_sample.shape
    depth = norm_g.shape[0]
    n_pages, page = page_table.shape[1], cache_b_k.shape[2]
    n_buf = cache_a_k.shape[2]
    win_p = min(n_buf, s)
    assert ds <= ROWS_S

    a_tiles = _a_prompt_bias_tiles(rel_bias)
    a_sample_bias = _a_sample_bias(rel_bias, n_buf, ds)
    c_tiles = _c_prompt_bias_tiles(rel_bias, s, min(s // 2, 256))
    c_sample_bias = _c_sample_bias(rel_bias, n_pages, page, ds)
    mem_k_all = cache_mem_k.reshape(depth, bs, MEM_LEN, X_W)
    mem_v_all = cache_mem_v.reshape(depth, bs, MEM_LEN, X_W)

    xp = x_prompt.reshape(bp * s, d)
    xs = jnp.pad(x_sample, ((0, 0), (0, ROWS_S - ds), (0, 0))).reshape(bs * ROWS_S, d)
    mem = mem_prompt.reshape(bp * MEM_LEN, d)
    outs = {name: [] for name in ("a_k_p", "a_v_p", "b_k_p", "b_v_p", "c_k_p", "c_v_p", "mem_k_p", "mem_v_p",
                                  "a_k_s", "a_v_s", "b_k_s", "b_v_s", "c_k_s", "c_v_s")}

    for l in range(depth):
        m, j = l % N_MIXERS, l // N_MIXERS
        w_in = (w_in_a, w_in_b, w_in_c)[m][j].astype(BF16)
        n_in = w_in.shape[1]
        n_mix = n_in - X_W - BRANCH_W
        zp = _norm_matmul(xp, norm_g[l], w_in)
        zs = _norm_matmul(xs, norm_g[l], w_in)
        mkv = _norm_matmul(mem, mem_norm_g[l], w_mem_kv[l].astype(BF16)).reshape(bp, MEM_LEN, 2 * X_W)
        outs["mem_k_p"].append(mkv[..., :X_W].reshape(bp, MEM_LEN, X_HEADS, X_HD))
        outs["mem_v_p"].append(mkv[..., X_W:].reshape(bp, MEM_LEN, X_HEADS, X_HD))
        zp3 = zp.reshape(bp, s, n_in)
        zs3 = zs.reshape(bs, ROWS_S, n_in)

        if m == 0:
            op = _a_prompt(zp3, a_tiles)
            os_ = _a_sample(zs3, cache_a_k, cache_a_v, j, a_sample_bias)
            heads = (A_KV_HEADS, A_HD)
            kcols, vcols = slice(A_Q, A_Q + A_KV), slice(A_Q + A_KV, A_Q + 2 * A_KV)
            outs["a_k_p"].append(zp3[:, s - win_p:, kcols].reshape(bp, win_p, *heads))
            outs["a_v_p"].append(zp3[:, s - win_p:, vcols].reshape(bp, win_p, *heads))
            outs["a_k_s"].append(zs3[:, :ds, kcols].reshape(bs, ds, *heads))
            outs["a_v_s"].append(zs3[:, :ds, vcols].reshape(bs, ds, *heads))
        elif m == 1:
            heads = (B_HEADS, B_HD)
            kcols, vcols = slice(B_W, 2 * B_W), slice(2 * B_W, 3 * B_W)
            op = _b_prompt(zp3)
            os_ = _b_sample(zs3, cache_b_k, cache_b_v, j, page_table)
            outs["b_k_p"].append(zp3[:, :, kcols].reshape(bp, s, *heads))
            outs["b_v_p"].append(zp3[:, :, vcols].reshape(bp, s, *heads))
            outs["b_k_s"].append(zs3[:, :ds, kcols].reshape(bs, ds, *heads))
            outs["b_v_s"].append(zs3[:, :ds, vcols].reshape(bs, ds, *heads))
        else:
            heads = (C_HEADS, C_VD)
            kcols, vcols = slice(C_W, 2 * C_W), slice(2 * C_W, 3 * C_W)
            lam0 = _lambda_init(l)
            op = _c_prompt(zp3, c_tiles, c_lambda[j], c_subln_g[j], lam0)
            os_ = _c_sample(zs3, cache_c_k, cache_c_v, j, page_table, c_sample_bias,
                            c_lambda[j], c_subln_g[j], lam0)
            outs["c_k_p"].append(zp3[:, :, kcols].reshape(bp, s, *heads))
            outs["c_v_p"].append(zp3[:, :, vcols].reshape(bp, s, *heads))
            outs["c_k_s"].append(zs3[:, :ds, kcols].reshape(bs, ds, *heads))
            outs["c_v_s"].append(zs3[:, :ds, vcols].reshape(bs, ds, *heads))

        q_col = n_mix // X_W
        cross_p = _cross_attend(zp3, q_col, mkv[None], 0, mkv[None], 1, 0, nb=1, rows=min(s, 512))
        cross_s = _cross_attend(zs3, q_col, mem_k_all, 0, mem_v_all, 0, l, nb=8, rows=ROWS_S)
        last = l == depth - 1
        xp = _gate_out(zp, q_col + 1, op.reshape(bp * s, MIX_W), cross_p.reshape(bp * s, X_W),
                       w_out[l].astype(BF16), xp, final_g, final=last)
        xs = _gate_out(zs, q_col + 1, os_.reshape(bs * ROWS_S, MIX_W), cross_s.reshape(bs * ROWS_S, X_W),
                       w_out[l].astype(BF16), xs, final_g, final=last)

    y_prompt = xp.reshape(bp, s, d)
    y_sample = xs.reshape(bs, ROWS_S, d)[:, :ds]
    stack = lambda name: jnp.stack(outs[name])
    return (y_prompt, y_sample,
            stack("a_k_p"), stack("a_v_p"), stack("b_k_p"), stack("b_v_p"), stack("c_k_p"), stack("c_v_p"),
            stack("mem_k_p"), stack("mem_v_p"),
            stack("a_k_s"), stack("a_v_s"), stack("b_k_s"), stack("b_v_s"), stack("c_k_s"), stack("c_v_s"))
```

```python
import functools
import math

import numpy as np
import jax
import jax.numpy as jnp
from jax import lax
from jax.experimental import pallas as pl
from jax.experimental.pallas import tpu as pltpu

F32 = jnp.float32
BF16 = jnp.bfloat16

EPS = 1e-6
NEG = -1e30
LANES = 128
SUBLANES = 8
VMEM_CAP = 60 * 1024 * 1024

N_MIXERS = 3
A_KV_HEADS = 8
A_HD = 128
A_DILATIONS = (1, 4, 16)
A_STEPS = 128
A_Q = len(A_DILATIONS) * A_KV_HEADS * A_HD
A_KV = A_KV_HEADS * A_HD
A_BLOCK = A_STEPS * A_DILATIONS[-1]
B_HEADS = 16
B_HD = 64
B_W = B_HEADS * B_HD
C_HEADS = 8
C_HD = 64
C_VD = 2 * C_HD
C_W = C_HEADS * C_VD
X_HEADS = 4
X_HD = 128
X_W = X_HEADS * X_HD
MEM_LEN = 256
MIX_W = 1024
BRANCH_W = MIX_W + X_W
NUM_BUCKETS = 32
MAX_EXACT = NUM_BUCKETS // 2
REL_MAX_DIST = 2048
C_BIAS_OFF = len(A_DILATIONS) * A_KV_HEADS
ROWS_S = SUBLANES


def _dot(a, b):
    return jnp.dot(a, b, preferred_element_type=F32)


def _dot_nt(a, b):
    return lax.dot_general(a, b, (((1,), (1,)), ((), ())), preferred_element_type=F32)


def _params(n_axes, vmem_bytes):
    return pltpu.CompilerParams(
        dimension_semantics=("arbitrary",) * n_axes,
        vmem_limit_bytes=int(min(VMEM_CAP, max(vmem_bytes, 16 * 1024 * 1024))))


def _rms_scale(x):
    return lax.rsqrt(jnp.mean(x * x, axis=-1, keepdims=True) + EPS)


def _split2(x):
    hi = x.astype(BF16)
    return hi, (x - hi.astype(F32)).astype(BF16)


LOG2_E = math.log2(math.e)


def _softplus2(z2):
    return jnp.maximum(z2, 0.0) + jnp.log2(1.0 + jnp.exp2(-jnp.abs(z2)))


def _norm_matmul_kernel(x_ref, g_ref, w_ref, o_ref, xn_ref):
    @pl.when(pl.program_id(1) == 0)
    def _():
        x = x_ref[...]
        xn_ref[...] = (x * _rms_scale(x) * g_ref[...]).astype(BF16)

    o_ref[...] = _dot(xn_ref[...], w_ref[...])


def _norm_matmul(x, g, w):
    t, d = x.shape
    n = w.shape[1]
    tm = min(t, 1024)
    tn = min(n, 1024)
    assert t % tm == 0 and n % tn == 0
    vmem = 2 * (tm * d * 4 + d * tn * 2 + tm * tn * 4) + tm * d * 2 + (4 << 20)
    return pl.pallas_call(
        _norm_matmul_kernel,
        grid=(t // tm, n // tn),
        in_specs=[pl.BlockSpec((tm, d), lambda i, j: (i, 0)),
                  pl.BlockSpec((1, d), lambda i, j: (0, 0)),
                  pl.BlockSpec((d, tn), lambda i, j: (0, j))],
        out_specs=pl.BlockSpec((tm, tn), lambda i, j: (i, j)),
        out_shape=jax.ShapeDtypeStruct((t, n), F32),
        scratch_shapes=[pltpu.VMEM((tm, d), BF16)],
        compiler_params=_params(2, vmem),
        name="norm_matmul",
    )(x, g.reshape(1, d), w)


def _cross_kernel(q_ref, mk_ref, mv_ref, o_ref, *, nb):
    scale = X_HD ** -0.5
    for n in range(nb):
        outs = []
        for h in range(X_HEADS):
            sl = slice(h * X_HD, (h + 1) * X_HD)
            s = _dot_nt(q_ref[n, :, sl].astype(BF16), mk_ref[0, n, :, sl].astype(BF16)) * scale
            p = jnp.exp(s - jnp.max(s, axis=-1, keepdims=True))
            l = jnp.sum(p, axis=-1, keepdims=True)
            outs.append(_dot(p.astype(BF16), mv_ref[0, n, :, sl].astype(BF16)) / l)
        o_ref[n] = jnp.concatenate(outs, axis=-1)


def _cross_attend(z3, q_col, mk, mk_col, mv, mv_col, layer, *, nb, rows):
    g, r, _ = z3.shape
    assert g % nb == 0 and r % rows == 0
    vmem = 2 * nb * (2 * rows * X_W * 4 + 2 * MEM_LEN * X_W * 4) + (8 << 20)
    return pl.pallas_call(
        functools.partial(_cross_kernel, nb=nb),
        grid=(g // nb, r // rows),
        in_specs=[pl.BlockSpec((nb, rows, X_W), lambda i, j: (i, j, q_col)),
                  pl.BlockSpec((1, nb, MEM_LEN, X_W), lambda i, j: (layer, i, 0, mk_col)),
                  pl.BlockSpec((1, nb, MEM_LEN, X_W), lambda i, j: (layer, i, 0, mv_col))],
        out_specs=pl.BlockSpec((nb, rows, X_W), lambda i, j: (i, j, 0)),
        out_shape=jax.ShapeDtypeStruct((g, r, X_W), F32),
        compiler_params=_params(2, vmem),
        name="cross_attend",
    )(z3, mk, mv)


def _cross_sample_kernel(q_ref, mk_ref, mv_ref, o_ref, *, nb):
    scale = X_HD ** -0.5
    n_rows = X_HEADS * ROWS_S
    own_head = (lax.broadcasted_iota(jnp.int32, (n_rows, MEM_LEN * X_HEADS), 0) // ROWS_S
                == lax.broadcasted_iota(jnp.int32, (n_rows, MEM_LEN * X_HEADS), 1) % X_HEADS)
    for n in range(nb):
        q = jnp.concatenate([q_ref[n, :, h * X_HD:(h + 1) * X_HD] for h in range(X_HEADS)], axis=0)
        s = jnp.where(own_head, _dot_nt(q.astype(BF16), mk_ref[0, n].astype(BF16)) * scale, NEG)
        p = jnp.exp(s - jnp.max(s, axis=-1, keepdims=True))
        l = jnp.sum(p, axis=-1, keepdims=True)
        o = _dot(p.astype(BF16), mv_ref[0, n].astype(BF16)) / l
        o_ref[n] = jnp.concatenate([o[h * ROWS_S:(h + 1) * ROWS_S] for h in range(X_HEADS)], axis=-1)


def _cross_attend_sample(z3, q_col, mem_k, mem_v, layer, *, nb):
    g = z3.shape[0]
    assert g % nb == 0 and z3.shape[1] == ROWS_S
    rows = MEM_LEN * X_HEADS
    vmem = 2 * nb * (2 * ROWS_S * X_W * 4 + 2 * rows * X_HD * 4) + (8 << 20)
    return pl.pallas_call(
        functools.partial(_cross_sample_kernel, nb=nb),
        grid=(g // nb,),
        in_specs=[pl.BlockSpec((nb, ROWS_S, X_W), lambda i: (i, 0, q_col)),
                  pl.BlockSpec((1, nb, rows, X_HD), lambda i: (layer, i, 0, 0)),
                  pl.BlockSpec((1, nb, rows, X_HD), lambda i: (layer, i, 0, 0))],
        out_specs=pl.BlockSpec((nb, ROWS_S, X_W), lambda i: (i, 0, 0)),
        out_shape=jax.ShapeDtypeStruct((g, ROWS_S, X_W), F32),
        compiler_params=_params(1, vmem),
        name="cross_attend_sample",
    )(z3, mem_k, mem_v)


def _silu(v):
    return v * (1.0 / (1.0 + jnp.exp(-v)))


def _gate_out_kernel(g0_ref, g1_ref, g2_ref, mix_ref, cross_ref, w_ref, x_ref, fg_ref, o_ref, *, final):
    half = MIX_W // 2
    u0 = (mix_ref[:, :half] * _silu(g0_ref[...])).astype(BF16)
    u1 = (mix_ref[:, half:] * _silu(g1_ref[...])).astype(BF16)
    u2 = (cross_ref[...] * _silu(g2_ref[...])).astype(BF16)
    y = x_ref[...] + (_dot(u0, w_ref[:half]) + _dot(u1, w_ref[half:MIX_W]) + _dot(u2, w_ref[MIX_W:]))
    if final:
        y = y * _rms_scale(y) * fg_ref[...]
    o_ref[...] = y


def _gate_out(z, gate_col, mix, cross, w_out, x, final_g, *, final):
    t, d = x.shape
    tm = min(t, 512)
    assert t % tm == 0 and X_W == MIX_W // 2
    vmem = 2 * tm * 4 * (3 * X_W + MIX_W + X_W + 2 * d) + 2 * BRANCH_W * d * 2 + (8 << 20)
    gate_spec = lambda c: pl.BlockSpec((tm, X_W), lambda i: (i, gate_col + c))
    return pl.pallas_call(
        functools.partial(_gate_out_kernel, final=final),
        grid=(t // tm,),
        in_specs=[gate_spec(0), gate_spec(1), gate_spec(2),
                  pl.BlockSpec((tm, MIX_W), lambda i: (i, 0)),
                  pl.BlockSpec((tm, X_W), lambda i: (i, 0)),
                  pl.BlockSpec((BRANCH_W, d), lambda i: (0, 0)),
                  pl.BlockSpec((tm, d), lambda i: (i, 0)),
                  pl.BlockSpec((1, d), lambda i: (0, 0))],
        out_specs=pl.BlockSpec((tm, d), lambda i: (i, 0)),
        out_shape=jax.ShapeDtypeStruct((t, d), F32),
        compiler_params=_params(1, vmem),
        name="gate_out",
    )(z, z, z, mix, cross, w_out, x, final_g.reshape(1, d))


def _rel_bucket(dist):
    n = jnp.maximum(dist, 0)
    nf = jnp.maximum(n, 1).astype(F32)
    large = MAX_EXACT + (jnp.log(nf / MAX_EXACT) / math.log(REL_MAX_DIST / MAX_EXACT)
                         * (NUM_BUCKETS - MAX_EXACT)).astype(jnp.int32)
    large = jnp.minimum(large, NUM_BUCKETS - 1)
    return jnp.where(n < MAX_EXACT, n, large)


def _bias_lookup(cols, dist, valid):
    bucket = _rel_bucket(jnp.asarray(dist, jnp.int32))[None]
    expand = (slice(None),) + (None,) * dist.ndim
    vals = jnp.zeros((cols.shape[1],) + dist.shape, F32)
    for b in range(NUM_BUCKETS):
        vals = jnp.where(bucket == b, cols[b].astype(F32)[expand], vals)
    return jnp.where(jnp.asarray(valid)[None], vals, NEG)


def _a_prompt_bias_tiles(rel_bias):
    back = np.arange(A_STEPS)[:, None] - (np.arange(2 * A_STEPS)[None, :] - A_STEPS)
    band = (back >= 0) & (back <= A_STEPS)
    return jnp.stack([_bias_lookup(rel_bias[:, g * A_KV_HEADS:(g + 1) * A_KV_HEADS],
                                   d * np.clip(back, 0, A_STEPS), band)
                      for g, d in enumerate(A_DILATIONS)])


def _c_prompt_bias_tiles(rel_bias, s, t):
    nd = s // t
    tile = jnp.arange(nd + 1, dtype=jnp.int32)[:, None, None]
    dist = tile * t + jnp.arange(t, dtype=jnp.int32)[None, :, None] - jnp.arange(t, dtype=jnp.int32)[None, None, :]
    dist = jnp.where(tile == nd, -1, dist)
    return _bias_lookup(rel_bias[:, C_BIAS_OFF:C_BIAS_OFF + C_HEADS], jnp.maximum(dist, 0), dist >= 0)


def _a_prompt_kernel(q0_ref, q1_ref, q2_ref, kc_ref, kp_ref, vc_ref, vp_ref, bias_ref, o_ref,
                     kk_ref, vv_ref, og_ref, lg_ref):
    blk = A_BLOCK
    first_block = pl.program_id(2) == 0
    kk_ref[:blk] = kp_ref[0]
    kk_ref[blk:] = kc_ref[0]
    vv_ref[:blk] = vp_ref[0]
    vv_ref[blk:] = vc_ref[0]
    col = lax.broadcasted_iota(jnp.int32, (A_STEPS, 2 * A_STEPS), 1)
    scale = A_HD ** -0.5

    def unit(u, g, d, q_ref):
        span = A_STEPS * d
        sb = u // d
        r = u - sb * d
        q0 = sb * span + r
        k0 = blk + q0 - span
        if d == 1:
            qs, ks = pl.ds(q0, A_STEPS), pl.ds(k0, 2 * A_STEPS)
        else:
            qs, ks = pl.ds(q0, A_STEPS, stride=d), pl.ds(k0, 2 * A_STEPS, stride=d)
        s = _dot_nt(q_ref[0, qs, :].astype(BF16), kk_ref[ks, :].astype(BF16)) * scale + bias_ref[g, 0]
        n_dead = jnp.where(jnp.logical_and(first_block, sb == 0), A_STEPS, 0)
        s = jnp.where(col < n_dead, NEG, s)
        m = jnp.max(s, axis=-1, keepdims=True)
        p = jnp.exp(s - m)
        l = jnp.sum(p, axis=-1, keepdims=True)
        og_ref[g, qs, :] = _dot(p.astype(BF16), vv_ref[ks, :].astype(BF16)) / l
        lg_ref[g, qs, :] = jnp.broadcast_to(m + jnp.log(l), (A_STEPS, A_HD))

    def units(u, _):
        for g, (d, q_ref) in enumerate(zip(A_DILATIONS, (q0_ref, q1_ref, q2_ref))):
            unit(u, g, d, q_ref)
        return 0

    lax.fori_loop(0, blk // A_STEPS, units, 0, unroll=8)

    l0, l1, l2 = lg_ref[0], lg_ref[1], lg_ref[2]
    m = jnp.maximum(jnp.maximum(l0, l1), l2)
    e0, e1, e2 = jnp.exp(l0 - m), jnp.exp(l1 - m), jnp.exp(l2 - m)
    den = e0 + e1 + e2
    o_ref[0] = (og_ref[0] * (e0 / den) + og_ref[1] * (e1 / den) + og_ref[2] * (e2 / den))


def _a_prompt(z3, bias_tiles):
    b, s, _ = z3.shape
    blk = A_BLOCK
    assert s % blk == 0
    h = A_KV_HEADS
    qspec = lambda g: pl.BlockSpec((1, blk, A_HD), lambda bi, hi, n: (bi, n, g * h + hi))
    kcol, vcol = A_Q // A_HD, (A_Q + A_KV) // A_HD
    cur = lambda c: pl.BlockSpec((1, blk, A_HD), lambda bi, hi, n: (bi, n, c + hi))
    prev = lambda c: pl.BlockSpec((1, blk, A_HD), lambda bi, hi, n: (bi, jnp.maximum(n - 1, 0), c + hi))
    vmem = 2 * 8 * blk * A_HD * 4 + 2 * 3 * A_STEPS * 2 * A_STEPS * 4 + (2 * 2 + 6) * blk * A_HD * 4 + (8 << 20)
    return pl.pallas_call(
        _a_prompt_kernel,
        grid=(b, h, s // blk),
        in_specs=[qspec(0), qspec(1), qspec(2), cur(kcol), prev(kcol), cur(vcol), prev(vcol),
                  pl.BlockSpec((len(A_DILATIONS), 1, A_STEPS, 2 * A_STEPS), lambda bi, hi, n: (0, hi, 0, 0))],
        out_specs=pl.BlockSpec((1, blk, A_HD), lambda bi, hi, n: (bi, n, hi)),
        out_shape=jax.ShapeDtypeStruct((b, s, A_KV), F32),
        scratch_shapes=[pltpu.VMEM((2 * blk, A_HD), F32), pltpu.VMEM((2 * blk, A_HD), F32),
                        pltpu.VMEM((len(A_DILATIONS), blk, A_HD), F32),
                        pltpu.VMEM((len(A_DILATIONS), blk, A_HD), F32)],
        compiler_params=_params(3, vmem),
        name="a_prompt",
    )(z3, z3, z3, z3, z3, z3, z3, bias_tiles)


def _later_matrix(t):
    return jnp.asarray(np.tril(np.ones((t, t), np.float32), -1), BF16)


def _stick_tile(z, later_mat, carry, valid):
    t = later_mat.shape[0]
    sp = _softplus2(z)
    lk = -sp
    if valid is not None:
        lk = jnp.where(valid, lk, 0.0)
    later = []
    for c in reversed(range(z.shape[1] // t)):
        hi, lo = _split2(lk[:, c * t:(c + 1) * t])
        later.append(_dot(hi, later_mat) + _dot(lo, later_mat) + carry)
        carry = carry + jnp.sum(lk[:, c * t:(c + 1) * t], axis=-1, keepdims=True)
    a = jnp.exp2(z - sp + jnp.concatenate(later[::-1], axis=1))
    if valid is not None:
        a = jnp.where(valid, a, 0.0)
    return a, carry


B_QUERY_ROWS = 512
B_KEY_RUN = 512


def _b_prompt_kernel(q_ref, k_ref, v_ref, t_ref, o_ref, *, tq, tk):
    qi = pl.program_id(2)
    q = q_ref[0] * (B_HD ** -0.5 * LOG2_E)
    lane = lax.broadcasted_iota(jnp.int32, (tq, 2 * B_HD), 1)
    q2 = jnp.concatenate([jnp.where(lane < B_HD, q, 0.0), jnp.where(lane >= B_HD, q, 0.0)], axis=0).astype(BF16)
    later_mat = t_ref[...]

    def tile(kb, carry, acc, valid):
        start = pl.multiple_of(kb * tk, tk)
        k = k_ref[0, pl.ds(start, tk), :].astype(BF16)
        v = v_ref[0, pl.ds(start, tk), :].astype(BF16)
        a, carry = _stick_tile(_dot_nt(q2, k), later_mat, carry, valid)
        return carry, acc + _dot(a.astype(BF16), v)

    top = (qi * tq) // tk
    row = lax.broadcasted_iota(jnp.int32, (2 * tq, tk), 0)
    col = lax.broadcasted_iota(jnp.int32, (2 * tq, tk), 1)
    strictly_earlier = col + top * tk < jnp.where(row >= tq, row - tq, row) + qi * tq
    carry, acc = tile(top, jnp.zeros((2 * tq, 1), F32), jnp.zeros((2 * tq, 2 * B_HD), F32), strictly_earlier)
    carry, acc = lax.fori_loop(0, top, lambda it, c: tile(top - 1 - it, c[0], c[1], None), (carry, acc))
    o_ref[0] = jnp.where(lane < B_HD, acc[:tq], acc[tq:])


def _b_prompt(z3):
    b, s, _ = z3.shape
    tk = min(s, B_KEY_RUN)
    tq = min(tk, B_QUERY_ROWS)
    assert s % tk == 0 and tk % tq == 0 and tk % 2 == 0
    pairs = B_W // LANES
    vmem = 2 * (2 * s * LANES * 4 + 2 * tq * LANES * 4) + tk * tk + 10 * 2 * tq * tk * 4 + (4 << 20)
    return pl.pallas_call(
        functools.partial(_b_prompt_kernel, tq=tq, tk=tk),
        grid=(b, pairs, s // tq),
        in_specs=[pl.BlockSpec((1, tq, LANES), lambda bi, hp, qi: (bi, qi, hp)),
                  pl.BlockSpec((1, s, LANES), lambda bi, hp, qi: (bi, 0, pairs + hp)),
                  pl.BlockSpec((1, s, LANES), lambda bi, hp, qi: (bi, 0, 2 * pairs + hp)),
                  pl.BlockSpec((tk // 2, tk // 2), lambda bi, hp, qi: (0, 0))],
        out_specs=pl.BlockSpec((1, tq, LANES), lambda bi, hp, qi: (bi, qi, hp)),
        out_shape=jax.ShapeDtypeStruct((b, s, B_W), F32),
        compiler_params=_params(3, vmem),
        name="b_prompt",
    )(z3, z3, z3, _later_matrix(tk // 2))


def _diff_lambda(lam_ref, lam0):
    t = lam_ref[...]
    a = jnp.sum(t[0:1] * t[1:2], axis=-1, keepdims=True)
    b = jnp.sum(t[2:3] * t[3:4], axis=-1, keepdims=True)
    return jnp.exp(a) - jnp.exp(b) + lam0


def _online_softmax_step(s, v, m, l, acc):
    m_new = jnp.maximum(m, jnp.max(s, axis=-1, keepdims=True))
    alpha = jnp.exp2(m - m_new)
    p = jnp.exp2(s - m_new)
    return m_new, alpha * l + jnp.sum(p, axis=-1, keepdims=True), alpha * acc + _dot(p.astype(BF16), v)


def _c_prompt_kernel(q_ref, k_ref, v_ref, bias_ref, lam_ref, g_ref, o_ref, *, tq, lam0):
    qi = pl.program_id(2)
    n_tiles = bias_ref.shape[1]
    q = q_ref[0] * (C_HD ** -0.5 * LOG2_E)
    lane = lax.broadcasted_iota(jnp.int32, (tq, C_VD), 1)
    q2 = jnp.concatenate([jnp.where(lane < C_HD, q, 0.0), jnp.where(lane >= C_HD, q, 0.0)], axis=0).astype(BF16)

    def body(kb2, c):
        start = pl.multiple_of(kb2 * 2 * tq, 2 * tq)
        k = k_ref[0, pl.ds(start, 2 * tq), :].astype(BF16)
        v = v_ref[0, pl.ds(start, 2 * tq), :].astype(BF16)
        nd0 = qi - 2 * kb2
        nd1 = jnp.where(nd0 == 0, n_tiles - 1, nd0 - 1)
        bias = jnp.concatenate([bias_ref[0, nd0], bias_ref[0, nd1]], axis=1)
        s = _dot_nt(q2, k) + jnp.concatenate([bias, bias], axis=0)
        return _online_softmax_step(s, v, *c)

    col0 = lambda fill: jnp.full((2 * tq, 1), fill, F32)
    m, l, acc = lax.fori_loop(0, qi // 2 + 1, body, (col0(NEG), col0(0.0), jnp.zeros((2 * tq, C_VD), F32)))
    norm = acc / l
    o = norm[:tq] - _diff_lambda(lam_ref, lam0) * norm[tq:]
    o_ref[0] = o * _rms_scale(o) * g_ref[...] * (1.0 - lam0)


def _c_prompt(z3, bias_tiles, c_lam, subln_g, lam0):
    b, s, _ = z3.shape
    tq = bias_tiles.shape[-1]
    nd = s // tq
    assert s % (2 * tq) == 0 and bias_tiles.shape[1] == nd + 1
    h = C_HEADS
    vmem = 2 * (2 * s * C_VD * 4 + 2 * tq * C_VD * 4 + (nd + 1) * tq * tq * 4) + (16 << 20)
    return pl.pallas_call(
        functools.partial(_c_prompt_kernel, tq=tq, lam0=lam0),
        grid=(b, h, nd),
        in_specs=[pl.BlockSpec((1, tq, C_VD), lambda bi, hi, qi: (bi, qi, hi)),
                  pl.BlockSpec((1, s, C_VD), lambda bi, hi, qi: (bi, 0, h + hi)),
                  pl.BlockSpec((1, s, C_VD), lambda bi, hi, qi: (bi, 0, 2 * h + hi)),
                  pl.BlockSpec((1, nd + 1, tq, tq), lambda bi, hi, qi: (hi, 0, 0, 0)),
                  pl.BlockSpec((4, C_HD), lambda bi, hi, qi: (0, 0)),
                  pl.BlockSpec((1, C_VD), lambda bi, hi, qi: (0, 0))],
        out_specs=pl.BlockSpec((1, tq, C_VD), lambda bi, hi, qi: (bi, qi, hi)),
        out_shape=jax.ShapeDtypeStruct((b, s, C_W), F32),
        compiler_params=_params(3, vmem),
        name="c_prompt",
    )(z3, z3, z3, bias_tiles, c_lam, subln_g.reshape(1, C_VD))


A_TAIL = 512
A_PICK = 4


def _a_sample_slots(n_buf):
    period = A_DILATIONS[-1]
    tail = np.arange(n_buf - A_TAIL, n_buf)
    picks = (np.arange((n_buf - A_TAIL) // period)[:, None] * period + np.arange(A_PICK)[None, :]).reshape(-1)
    new = np.where(np.arange(LANES) < ROWS_S, n_buf + np.arange(LANES), -1)
    return np.concatenate([tail, picks, new])


def _a_sample_bias(rel_bias, n_buf, ds):
    pos = _a_sample_slots(n_buf)
    i = np.arange(ROWS_S)[:, None]
    delta = n_buf + i - pos[None, :]
    groups = []
    for g, d in enumerate(A_DILATIONS):
        ok = (pos[None] >= 0) & (delta >= 0) & (delta % d == 0) & (delta // d <= A_STEPS) & (i < ds)
        ok = ok | ((i >= ds) & (pos[None] == n_buf + i))
        groups.append(_bias_lookup(rel_bias[:, g * A_KV_HEADS:(g + 1) * A_KV_HEADS],
                                   np.clip(delta, 0, d * A_STEPS), ok))
    return jnp.concatenate(groups, axis=1)


def _a_sample_kernel(q_ref, kn_ref, vn_ref, kt_ref, ks_ref, vt_ref, vs_ref, bias_ref, o_ref,
                     knp_ref, vnp_ref, *, n_tail, n_pick):
    scale = A_HD ** -0.5
    n_groups = len(A_DILATIONS)
    knp_ref[...] = jnp.zeros(knp_ref.shape, F32)
    vnp_ref[...] = jnp.zeros(vnp_ref.shape, F32)
    tail_rows = A_TAIL // n_tail
    pick_m = LANES // A_PICK
    for h in range(A_KV_HEADS):
        sl = slice(h * A_HD, (h + 1) * A_HD)
        q = jnp.concatenate([q_ref[0, :, (g * A_KV_HEADS + h) * A_HD:(g * A_KV_HEADS + h + 1) * A_HD]
                             for g in range(n_groups)], axis=0).astype(BF16)
        knp_ref[:ROWS_S] = kn_ref[0, :, sl]
        vnp_ref[:ROWS_S] = vn_ref[0, :, sl]

        def slab(ref, i):
            if i < n_tail:
                m0 = i * tail_rows // 16
                x = ref[0][0, 0, m0:m0 + tail_rows // 16, pl.ds(h, 16, stride=SUBLANES), :]
            elif i < n_tail + n_pick:
                m0 = (i - n_tail) * pick_m
                x = ref[1][0, 0, m0:m0 + pick_m, pl.ds(h, A_PICK, stride=SUBLANES), :]
            else:
                return ref[2][...].astype(BF16)
            return x.reshape(LANES, A_HD).astype(BF16)

        n_slabs = n_tail + n_pick + 1
        krefs, vrefs = (kt_ref, ks_ref, knp_ref), (vt_ref, vs_ref, vnp_ref)
        s = jnp.concatenate([_dot_nt(q, slab(krefs, i)) for i in range(n_slabs)], axis=-1)
        s = s * scale + bias_ref[h]
        m = jnp.max(s, axis=-1, keepdims=True)
        p = jnp.exp(s - m)
        l = jnp.sum(p, axis=-1, keepdims=True)
        pb = p.astype(BF16)
        o = _dot(pb[:, :LANES], slab(vrefs, 0))
        for i in range(1, n_slabs):
            o = o + _dot(pb[:, i * LANES:(i + 1) * LANES], slab(vrefs, i))
        o = o / l
        lse = m + jnp.log(l)
        rows = lambda x, g: x[g * ROWS_S:(g + 1) * ROWS_S]
        mx = jnp.maximum(jnp.maximum(rows(lse, 0), rows(lse, 1)), rows(lse, 2))
        e = [jnp.exp(rows(lse, g) - mx) for g in range(n_groups)]
        den = e[0] + e[1] + e[2]
        o_ref[0, :, sl] = sum(rows(o, g) * (e[g] / den) for g in range(n_groups))


def _a_sample(zs3, cache_k, cache_v, j, bias):
    bs = zs3.shape[0]
    n_buf = cache_k.shape[2]
    period = A_DILATIONS[-1]
    assert n_buf % LANES == 0 and n_buf >= A_TAIL and (n_buf - A_TAIL) % (period * LANES // A_PICK) == 0
    n_per = n_buf // period
    tail_per = A_TAIL // period
    n_tail = A_TAIL // LANES
    n_pick = (n_per - tail_per) * A_PICK // LANES
    rows_per = period * A_KV_HEADS
    view = lambda c: c.reshape(c.shape[0], bs, n_per, rows_per, A_HD)
    tail = pl.BlockSpec((1, 1, tail_per, rows_per, A_HD), lambda b: (j, b, n_per // tail_per - 1, 0, 0))
    pick = pl.BlockSpec((1, 1, n_per - tail_per, A_PICK * A_KV_HEADS, A_HD), lambda b: (j, b, 0, 0, 0))
    n_slots = (n_tail + n_pick + 1) * LANES
    n_rows = len(A_DILATIONS) * ROWS_S
    vmem = 2 * 2 * (A_TAIL + (n_per - tail_per) * A_PICK) * A_KV_HEADS * A_HD * 4 + (12 << 20)
    kcol, vcol = A_Q // A_KV, A_Q // A_KV + 1
    return pl.pallas_call(
        functools.partial(_a_sample_kernel, n_tail=n_tail, n_pick=n_pick),
        grid=(bs,),
        in_specs=[pl.BlockSpec((1, ROWS_S, A_Q), lambda b: (b, 0, 0)),
                  pl.BlockSpec((1, ROWS_S, A_KV), lambda b: (b, 0, kcol)),
                  pl.BlockSpec((1, ROWS_S, A_KV), lambda b: (b, 0, vcol)),
                  tail, pick, tail, pick,
                  pl.BlockSpec((A_KV_HEADS, n_rows, n_slots), lambda b: (0, 0, 0))],
        out_specs=pl.BlockSpec((1, ROWS_S, A_KV), lambda b: (b, 0, 0)),
        out_shape=jax.ShapeDtypeStruct((bs, ROWS_S, A_KV), F32),
        scratch_shapes=[pltpu.VMEM((LANES, A_HD), F32), pltpu.VMEM((LANES, A_HD), F32)],
        compiler_params=_params(1, vmem),
        name="a_sample",
    )(zs3, zs3, zs3, view(cache_k), view(cache_k), view(cache_v), view(cache_v), bias)


PAGES_PER_STEP = 8


def _b_sample_kernel(pt_ref, q_ref, kn_ref, vn_ref, *refs, page, n_sub):
    del pt_ref
    kp_refs, vp_refs = refs[:n_sub], refs[n_sub:2 * n_sub]
    t_ref, o_ref, qbd_ref, knp_ref, vnp_ref, carry_ref, acc_ref = refs[2 * n_sub:]
    step = pl.program_id(1)
    later_mat = t_ref[...]
    n_rows = B_HEADS * ROWS_S
    own_head = (lax.broadcasted_iota(jnp.int32, (n_rows, B_W), 0) // ROWS_S
                == lax.broadcasted_iota(jnp.int32, (n_rows, B_W), 1) // B_HD)

    def attend(z, v_nt, valid):
        a, carry = _stick_tile(z, later_mat, carry_ref[...], valid)
        carry_ref[...] = carry
        ab = a.astype(BF16)
        acc_ref[...] += _dot(ab, vnp_ref[...].astype(BF16)) if v_nt is None else _dot_nt(ab, v_nt)

    @pl.when(step == 0)
    def _():
        carry_ref[...] = jnp.zeros(carry_ref.shape, F32)
        acc_ref[...] = jnp.zeros(acc_ref.shape, F32)
        knp_ref[...] = jnp.zeros(knp_ref.shape, F32)
        vnp_ref[...] = jnp.zeros(vnp_ref.shape, F32)
        knp_ref[:ROWS_S] = kn_ref[0]
        vnp_ref[:ROWS_S] = vn_ref[0]
        q = q_ref[0] * (B_HD ** -0.5 * LOG2_E)
        qbd = jnp.where(own_head, jnp.concatenate([q] * B_HEADS, axis=0), 0.0).astype(BF16)
        qbd_ref[...] = qbd
        i8 = lax.broadcasted_iota(jnp.int32, (n_rows, page), 0) % ROWS_S
        jj = lax.broadcasted_iota(jnp.int32, (n_rows, page), 1)
        attend(_dot_nt(qbd, knp_ref[...].astype(BF16)), None, jj < i8)

    nt = lambda page_refs: jnp.concatenate([r[0, 0].reshape(B_W, page).astype(BF16) for r in page_refs], axis=1)
    attend(_dot(qbd_ref[...], nt(kp_refs)), nt(vp_refs), None)

    @pl.when(step == pl.num_programs(1) - 1)
    def _():
        own = jnp.where(own_head, acc_ref[...], 0.0).reshape(B_HEADS, ROWS_S, B_W)
        o_ref[0] = jnp.sum(own, axis=0)


def _b_sample(zs3, cache_k, cache_v, j, page_table):
    bs = zs3.shape[0]
    n_pages = page_table.shape[1]
    page = cache_k.shape[2]
    n_sub = PAGES_PER_STEP
    assert page == LANES and n_pages % n_sub == 0
    view = lambda c: jnp.transpose(c, (0, 1, 3, 4, 2))
    page_spec = lambda u: pl.BlockSpec(
        (1, 1, B_HEADS, B_HD, page), lambda b, s, pt: (j, pt[b, n_pages - n_sub * (s + 1) + u], 0, 0, 0))
    kcol, vcol = 1, 2
    n_rows = B_HEADS * ROWS_S
    vmem = 2 * 2 * n_sub * B_W * page * 4 + 4 * page * B_W * 4 + (12 << 20)
    grid_spec = pltpu.PrefetchScalarGridSpec(
        num_scalar_prefetch=1,
        grid=(bs, n_pages // n_sub),
        in_specs=[pl.BlockSpec((1, ROWS_S, B_W), lambda b, s, pt: (b, 0, 0)),
                  pl.BlockSpec((1, ROWS_S, B_W), lambda b, s, pt: (b, 0, kcol)),
                  pl.BlockSpec((1, ROWS_S, B_W), lambda b, s, pt: (b, 0, vcol))]
                 + [page_spec(u) for u in range(n_sub)] * 2
                 + [pl.BlockSpec((page, page), lambda b, s, pt: (0, 0))],
        out_specs=pl.BlockSpec((1, ROWS_S, B_W), lambda b, s, pt: (b, 0, 0)),
        scratch_shapes=[pltpu.VMEM((n_rows, B_W), BF16),
                        pltpu.VMEM((page, B_W), F32), pltpu.VMEM((page, B_W), F32),
                        pltpu.VMEM((n_rows, 1), F32), pltpu.VMEM((n_rows, B_W), F32)])
    kt, vt = view(cache_k), view(cache_v)
    return pl.pallas_call(
        functools.partial(_b_sample_kernel, page=page, n_sub=n_sub),
        grid_spec=grid_spec,
        out_shape=jax.ShapeDtypeStruct((bs, ROWS_S, B_W), F32),
        compiler_params=_params(2, vmem),
        name="b_sample",
    )(page_table, zs3, zs3, zs3, *([kt] * n_sub), *([vt] * n_sub), _later_matrix(page))


def _c_sample_bias(rel_bias, n_pages, page, ds):
    past = n_pages * page
    i = np.arange(ROWS_S)
    dist_past = past + i[None, :, None] - (np.arange(n_pages)[:, None, None] * page + np.arange(page)[None, None, :])
    dist_new = i[:, None] - np.arange(page)[None, :]
    ok_new = (dist_new >= 0) & (np.arange(page)[None, :] < max(ds, 1))
    dist = np.concatenate([dist_past, np.maximum(dist_new, 0)[None]])
    ok = np.concatenate([np.ones_like(dist_past, bool), ok_new[None]])
    tiles = _bias_lookup(rel_bias[:, C_BIAS_OFF:C_BIAS_OFF + C_HEADS], dist, ok)
    tiles = jnp.broadcast_to(tiles[:, :, None], (C_HEADS, n_pages + 1, 2, ROWS_S, page))
    return jnp.moveaxis(tiles, 1, 0).reshape(n_pages + 1, C_HEADS * 2 * ROWS_S, page)


def _c_sample_kernel(pt_ref, q_ref, kn_ref, vn_ref, *refs, page, lam0, n_sub):
    del pt_ref
    kp_refs, vp_refs = refs[:n_sub], refs[n_sub:2 * n_sub]
    bnew_ref, bias_ref, lam_ref, g_ref, o_ref, knp_ref, vnp_ref, m_ref, l_ref, acc_ref = refs[2 * n_sub:]
    step = pl.program_id(1)
    rows_h = 2 * ROWS_S
    lane = lax.broadcasted_iota(jnp.int32, (ROWS_S, C_VD), 1)

    def q_head(h):
        q = q_ref[0, :, h * C_VD:(h + 1) * C_VD] * (C_HD ** -0.5)
        return jnp.concatenate([jnp.where(lane < C_HD, q, 0.0), jnp.where(lane >= C_HD, q, 0.0)],
                               axis=0).astype(BF16)

    def paged(page_refs):
        return lambda h: jnp.concatenate(
            [r[0, 0, pl.ds(h, page, stride=C_HEADS), :] for r in page_refs], axis=0).astype(BF16)

    def fresh(ref):
        return lambda h: ref[:, h * C_VD:(h + 1) * C_VD].astype(BF16)

    def attend(keys_of, vals_of, bias):
        s = jnp.concatenate([_dot_nt(q_head(h), keys_of(h)) for h in range(C_HEADS)], axis=0) + bias
        m_old = m_ref[...]
        m_new = jnp.maximum(m_old, jnp.max(s, axis=-1, keepdims=True))
        alpha = jnp.exp(m_old - m_new)
        p = jnp.exp(s - m_new)
        m_ref[...] = m_new
        l_ref[...] = alpha * l_ref[...] + jnp.sum(p, axis=-1, keepdims=True)
        pb = p.astype(BF16)
        for h in range(C_HEADS):
            rs = slice(h * rows_h, (h + 1) * rows_h)
            acc_ref[rs, :] = alpha[rs] * acc_ref[rs, :] + _dot(pb[rs], vals_of(h))

    @pl.when(step == 0)
    def _():
        m_ref[...] = jnp.full(m_ref.shape, NEG, F32)
        l_ref[...] = jnp.zeros(l_ref.shape, F32)
        acc_ref[...] = jnp.zeros(acc_ref.shape, F32)
        knp_ref[...] = jnp.zeros(knp_ref.shape, F32)
        vnp_ref[...] = jnp.zeros(vnp_ref.shape, F32)
        knp_ref[:ROWS_S] = kn_ref[0]
        vnp_ref[:ROWS_S] = vn_ref[0]
        attend(fresh(knp_ref), fresh(vnp_ref), bnew_ref[0])

    attend(paged(kp_refs), paged(vp_refs), jnp.concatenate([bias_ref[u] for u in range(n_sub)], axis=1))

    @pl.when(step == pl.num_programs(1) - 1)
    def _():
        lam = _diff_lambda(lam_ref, lam0)
        norm = acc_ref[...] / l_ref[...]
        outs = []
        for h in range(C_HEADS):
            o = norm[h * rows_h:h * rows_h + ROWS_S] - lam * norm[h * rows_h + ROWS_S:(h + 1) * rows_h]
            outs.append(o * _rms_scale(o) * g_ref[...] * (1.0 - lam0))
        o_ref[0] = jnp.concatenate(outs, axis=-1)


def _c_sample(zs3, cache_k, cache_v, j, page_table, bias, c_lam, subln_g, lam0):
    bs = zs3.shape[0]
    n_pages = page_table.shape[1]
    page = cache_k.shape[2]
    n_sub = PAGES_PER_STEP
    assert page == LANES and n_pages % n_sub == 0
    rows = page * C_HEADS
    n_rows = C_HEADS * 2 * ROWS_S
    view = lambda c: c.reshape(c.shape[0], c.shape[1], rows, C_VD)
    page_spec = lambda u: pl.BlockSpec((1, 1, rows, C_VD), lambda b, s, pt: (j, pt[b, n_sub * s + u], 0, 0))
    kcol, vcol = 1, 2
    vmem = 2 * 2 * n_sub * rows * C_VD * 4 + 2 * page * C_W * 4 + (12 << 20)
    grid_spec = pltpu.PrefetchScalarGridSpec(
        num_scalar_prefetch=1,
        grid=(bs, n_pages // n_sub),
        in_specs=[pl.BlockSpec((1, ROWS_S, C_W), lambda b, s, pt: (b, 0, 0)),
                  pl.BlockSpec((1, ROWS_S, C_W), lambda b, s, pt: (b, 0, kcol)),
                  pl.BlockSpec((1, ROWS_S, C_W), lambda b, s, pt: (b, 0, vcol))]
                 + [page_spec(u) for u in range(n_sub)] * 2
                 + [pl.BlockSpec((1, n_rows, page), lambda b, s, pt: (n_pages, 0, 0)),
                    pl.BlockSpec((n_sub, n_rows, page), lambda b, s, pt: (s, 0, 0)),
                    pl.BlockSpec((4, C_HD), lambda b, s, pt: (0, 0)),
                    pl.BlockSpec((1, C_VD), lambda b, s, pt: (0, 0))],
        out_specs=pl.BlockSpec((1, ROWS_S, C_W), lambda b, s, pt: (b, 0, 0)),
        scratch_shapes=[pltpu.VMEM((page, C_W), F32), pltpu.VMEM((page, C_W), F32),
                        pltpu.VMEM((n_rows, 1), F32), pltpu.VMEM((n_rows, 1), F32),
                        pltpu.VMEM((n_rows, C_VD), F32)])
    kc, vc = view(cache_k), view(cache_v)
    return pl.pallas_call(
        functools.partial(_c_sample_kernel, page=page, lam0=lam0, n_sub=n_sub),
        grid_spec=grid_spec,
        out_shape=jax.ShapeDtypeStruct((bs, ROWS_S, C_W), F32),
        compiler_params=_params(2, vmem),
        name="c_sample",
    )(page_table, zs3, zs3, zs3, *([kc] * n_sub), *([vc] * n_sub), bias, bias, c_lam, subln_g.reshape(1, C_VD))


def _lambda_init(layer):
    return 0.8 - 0.6 * math.exp(-0.3 * layer)


def kernel(x_prompt, x_sample, cache_a_k, cache_a_v, cache_b_k, cache_b_v, cache_c_k, cache_c_v,
           cache_mem_k, cache_mem_v, page_table, mem_prompt, norm_g, final_g, mem_norm_g, w_mem_kv,
           rel_bias, w_in_a, w_in_b, w_in_c, c_lambda, c_subln_g, w_out):
    bp, s, d = x_prompt.shape
    bs, ds, _ = x_sample.shape
    depth = norm_g.shape[0]
    n_pages, page = page_table.shape[1], cache_b_k.shape[2]
    n_buf = cache_a_k.shape[2]
    win_p = min(n_buf, s)
    assert ds <= ROWS_S

    a_tiles = _a_prompt_bias_tiles(rel_bias)
    a_sample_bias = _a_sample_bias(rel_bias, n_buf, ds)
    c_tiles = _c_prompt_bias_tiles(rel_bias, s, min(s // 2, 256)) * LOG2_E
    c_sample_bias = _c_sample_bias(rel_bias, n_pages, page, ds)
    mem_k_all = cache_mem_k.reshape(depth, bs, MEM_LEN * X_HEADS, X_HD)
    mem_v_all = cache_mem_v.reshape(depth, bs, MEM_LEN * X_HEADS, X_HD)

    xp = x_prompt.reshape(bp * s, d)
    xs = jnp.pad(x_sample, ((0, 0), (0, ROWS_S - ds), (0, 0))).reshape(bs * ROWS_S, d)
    mem = mem_prompt.reshape(bp * MEM_LEN, d)
    outs = {name: [] for name in ("a_k_p", "a_v_p", "b_k_p", "b_v_p", "c_k_p", "c_v_p", "mem_k_p", "mem_v_p",
                                  "a_k_s", "a_v_s", "b_k_s", "b_v_s", "c_k_s", "c_v_s")}

    for l in range(depth):
        m, j = l % N_MIXERS, l // N_MIXERS
        w_in = (w_in_a, w_in_b, w_in_c)[m][j].astype(BF16)
        n_in = w_in.shape[1]
        n_mix = n_in - X_W - BRANCH_W
        zp = _norm_matmul(xp, norm_g[l], w_in)
        zs = _norm_matmul(xs, norm_g[l], w_in)
        mkv = _norm_matmul(mem, mem_norm_g[l], w_mem_kv[l].astype(BF16)).reshape(bp, MEM_LEN, 2 * X_W)
        outs["mem_k_p"].append(mkv[..., :X_W].reshape(bp, MEM_LEN, X_HEADS, X_HD))
        outs["mem_v_p"].append(mkv[..., X_W:].reshape(bp, MEM_LEN, X_HEADS, X_HD))
        zp3 = zp.reshape(bp, s, n_in)
        zs3 = zs.reshape(bs, ROWS_S, n_in)

        if m == 0:
            op = _a_prompt(zp3, a_tiles)
            os_ = _a_sample(zs3, cache_a_k, cache_a_v, j, a_sample_bias)
            heads = (A_KV_HEADS, A_HD)
            kcols, vcols = slice(A_Q, A_Q + A_KV), slice(A_Q + A_KV, A_Q + 2 * A_KV)
            outs["a_k_p"].append(zp3[:, s - win_p:, kcols].reshape(bp, win_p, *heads))
            outs["a_v_p"].append(zp3[:, s - win_p:, vcols].reshape(bp, win_p, *heads))
            outs["a_k_s"].append(zs3[:, :ds, kcols].reshape(bs, ds, *heads))
            outs["a_v_s"].append(zs3[:, :ds, vcols].reshape(bs, ds, *heads))
        elif m == 1:
            heads = (B_HEADS, B_HD)
            kcols, vcols = slice(B_W, 2 * B_W), slice(2 * B_W, 3 * B_W)
            op = _b_prompt(zp3)
            os_ = _b_sample(zs3, cache_b_k, cache_b_v, j, page_table)
            outs["b_k_p"].append(zp3[:, :, kcols].reshape(bp, s, *heads))
            outs["b_v_p"].append(zp3[:, :, vcols].reshape(bp, s, *heads))
            outs["b_k_s"].append(zs3[:, :ds, kcols].reshape(bs, ds, *heads))
            outs["b_v_s"].append(zs3[:, :ds, vcols].reshape(bs, ds, *heads))
        else:
            heads = (C_HEADS, C_VD)
            kcols, vcols = slice(C_W, 2 * C_W), slice(2 * C_W, 3 * C_W)
            lam0 = _lambda_init(l)
            op = _c_prompt(zp3, c_tiles, c_lambda[j], c_subln_g[j], lam0)
            os_ = _c_sample(zs3, cache_c_k, cache_c_v, j, page_table, c_sample_bias,
                            c_lambda[j], c_subln_g[j], lam0)
            outs["c_k_p"].append(zp3[:, :, kcols].reshape(bp, s, *heads))
            outs["c_v_p"].append(zp3[:, :, vcols].reshape(bp, s, *heads))
            outs["c_k_s"].append(zs3[:, :ds, kcols].reshape(bs, ds, *heads))
            outs["c_v_s"].append(zs3[:, :ds, vcols].reshape(bs, ds, *heads))

        q_col = n_mix // X_W
        cross_p = _cross_attend(zp3, q_col, mkv[None], 0, mkv[None], 1, 0, nb=1, rows=min(s, 512))
        cross_s = _cross_attend_sample(zs3, q_col, mem_k_all, mem_v_all, l, nb=8)
        last = l == depth - 1
        xp = _gate_out(zp, q_col + 1, op.reshape(bp * s, MIX_W), cross_p.reshape(bp * s, X_W),
                       w_out[l].astype(BF16), xp, final_g, final=last)
        xs = _gate_out(zs, q_col + 1, os_.reshape(bs * ROWS_S, MIX_W), cross_s.reshape(bs * ROWS_S, X_W),
                       w_out[l].astype(BF16), xs, final_g, final=last)

    y_prompt = xp.reshape(bp, s, d)
    y_sample = xs.reshape(bs, ROWS_S, d)[:, :ds]
    stack = lambda name: jnp.stack(outs[name])
    return (y_prompt, y_sample,
            stack("a_k_p"), stack("a_v_p"), stack("b_k_p"), stack("b_v_p"), stack("c_k_p"), stack("c_v_p"),
            stack("mem_k_p"), stack("mem_v_p"),
            stack("a_k_s"), stack("a_v_s"), stack("b_k_s"), stack("b_v_s"), stack("c_k_s"), stack("c_v_s"))
```

```python
import functools
import math

import numpy as np
import jax
import jax.numpy as jnp
from jax import lax
from jax.experimental import pallas as pl
from jax.experimental.pallas import tpu as pltpu

F32 = jnp.float32
BF16 = jnp.bfloat16

EPS = 1e-6
NEG = -1e30
LANES = 128
SUBLANES = 8
VMEM_CAP = 60 * 1024 * 1024

N_MIXERS = 3
A_KV_HEADS = 8
A_HD = 128
A_DILATIONS = (1, 4, 16)
A_STEPS = 128
A_Q = len(A_DILATIONS) * A_KV_HEADS * A_HD
A_KV = A_KV_HEADS * A_HD
A_BLOCK = A_STEPS * A_DILATIONS[-1]
B_HEADS = 16
B_HD = 64
B_W = B_HEADS * B_HD
C_HEADS = 8
C_HD = 64
C_VD = 2 * C_HD
C_W = C_HEADS * C_VD
X_HEADS = 4
X_HD = 128
X_W = X_HEADS * X_HD
MEM_LEN = 256
MIX_W = 1024
BRANCH_W = MIX_W + X_W
NUM_BUCKETS = 32
MAX_EXACT = NUM_BUCKETS // 2
REL_MAX_DIST = 2048
C_BIAS_OFF = len(A_DILATIONS) * A_KV_HEADS
ROWS_S = SUBLANES


def _dot(a, b):
    return jnp.dot(a, b, preferred_element_type=F32)


def _dot_nt(a, b):
    return lax.dot_general(a, b, (((1,), (1,)), ((), ())), preferred_element_type=F32)


def _params(n_axes, vmem_bytes):
    return pltpu.CompilerParams(
        dimension_semantics=("arbitrary",) * n_axes,
        vmem_limit_bytes=int(min(VMEM_CAP, max(vmem_bytes, 16 * 1024 * 1024))))


def _rms_scale(x):
    return lax.rsqrt(jnp.mean(x * x, axis=-1, keepdims=True) + EPS)


def _split2(x):
    hi = x.astype(BF16)
    return hi, (x - hi.astype(F32)).astype(BF16)


LOG2_E = math.log2(math.e)


def _softplus2(z2):
    return jnp.maximum(z2, 0.0) + jnp.log2(1.0 + jnp.exp2(-jnp.abs(z2)))


def _norm_matmul_kernel(x_ref, g_ref, w_ref, o_ref, xn_ref):
    @pl.when(pl.program_id(1) == 0)
    def _():
        x = x_ref[...]
        xn_ref[...] = (x * _rms_scale(x) * g_ref[...]).astype(BF16)

    o_ref[...] = _dot(xn_ref[...], w_ref[...])


def _norm_matmul(x, g, w):
    t, d = x.shape
    n = w.shape[1]
    tm = min(t, 1024)
    tn = min(n, 1024)
    assert t % tm == 0 and n % tn == 0
    vmem = 2 * (tm * d * 4 + d * tn * 2 + tm * tn * 4) + tm * d * 2 + (4 << 20)
    return pl.pallas_call(
        _norm_matmul_kernel,
        grid=(t // tm, n // tn),
        in_specs=[pl.BlockSpec((tm, d), lambda i, j: (i, 0)),
                  pl.BlockSpec((1, d), lambda i, j: (0, 0)),
                  pl.BlockSpec((d, tn), lambda i, j: (0, j))],
        out_specs=pl.BlockSpec((tm, tn), lambda i, j: (i, j)),
        out_shape=jax.ShapeDtypeStruct((t, n), F32),
        scratch_shapes=[pltpu.VMEM((tm, d), BF16)],
        compiler_params=_params(2, vmem),
        name="norm_matmul",
    )(x, g.reshape(1, d), w)


def _cross_kernel(q_ref, mk_ref, mv_ref, o_ref, *, nb):
    scale = X_HD ** -0.5
    for n in range(nb):
        outs = []
        for h in range(X_HEADS):
            sl = slice(h * X_HD, (h + 1) * X_HD)
            s = _dot_nt(q_ref[n, :, sl].astype(BF16), mk_ref[0, n, :, sl].astype(BF16)) * scale
            p = jnp.exp(s - jnp.max(s, axis=-1, keepdims=True))
            l = jnp.sum(p, axis=-1, keepdims=True)
            outs.append(_dot(p.astype(BF16), mv_ref[0, n, :, sl].astype(BF16)) / l)
        o_ref[n] = jnp.concatenate(outs, axis=-1)


def _cross_attend(z3, q_col, mk, mk_col, mv, mv_col, layer, *, nb, rows):
    g, r, _ = z3.shape
    assert g % nb == 0 and r % rows == 0
    vmem = 2 * nb * (2 * rows * X_W * 4 + 2 * MEM_LEN * X_W * 4) + (8 << 20)
    return pl.pallas_call(
        functools.partial(_cross_kernel, nb=nb),
        grid=(g // nb, r // rows),
        in_specs=[pl.BlockSpec((nb, rows, X_W), lambda i, j: (i, j, q_col)),
                  pl.BlockSpec((1, nb, MEM_LEN, X_W), lambda i, j: (layer, i, 0, mk_col)),
                  pl.BlockSpec((1, nb, MEM_LEN, X_W), lambda i, j: (layer, i, 0, mv_col))],
        out_specs=pl.BlockSpec((nb, rows, X_W), lambda i, j: (i, j, 0)),
        out_shape=jax.ShapeDtypeStruct((g, r, X_W), F32),
        compiler_params=_params(2, vmem),
        name="cross_attend",
    )(z3, mk, mv)


def _cross_sample_kernel(q_ref, mk_ref, mv_ref, o_ref, *, nb):
    scale = X_HD ** -0.5
    n_rows = X_HEADS * ROWS_S
    own_head = (lax.broadcasted_iota(jnp.int32, (n_rows, MEM_LEN * X_HEADS), 0) // ROWS_S
                == lax.broadcasted_iota(jnp.int32, (n_rows, MEM_LEN * X_HEADS), 1) % X_HEADS)
    for n in range(nb):
        q = jnp.concatenate([q_ref[n, :, h * X_HD:(h + 1) * X_HD] for h in range(X_HEADS)], axis=0)
        s = jnp.where(own_head, _dot_nt(q.astype(BF16), mk_ref[0, n].astype(BF16)) * scale, NEG)
        p = jnp.exp(s - jnp.max(s, axis=-1, keepdims=True))
        l = jnp.sum(p, axis=-1, keepdims=True)
        o = _dot(p.astype(BF16), mv_ref[0, n].astype(BF16)) / l
        o_ref[n] = jnp.concatenate([o[h * ROWS_S:(h + 1) * ROWS_S] for h in range(X_HEADS)], axis=-1)


def _cross_attend_sample(z3, q_col, mem_k, mem_v, layer, *, nb):
    g = z3.shape[0]
    assert g % nb == 0 and z3.shape[1] == ROWS_S
    rows = MEM_LEN * X_HEADS
    vmem = 2 * nb * (2 * ROWS_S * X_W * 4 + 2 * rows * X_HD * 4) + (8 << 20)
    return pl.pallas_call(
        functools.partial(_cross_sample_kernel, nb=nb),
        grid=(g // nb,),
        in_specs=[pl.BlockSpec((nb, ROWS_S, X_W), lambda i: (i, 0, q_col)),
                  pl.BlockSpec((1, nb, rows, X_HD), lambda i: (layer, i, 0, 0)),
                  pl.BlockSpec((1, nb, rows, X_HD), lambda i: (layer, i, 0, 0))],
        out_specs=pl.BlockSpec((nb, ROWS_S, X_W), lambda i: (i, 0, 0)),
        out_shape=jax.ShapeDtypeStruct((g, ROWS_S, X_W), F32),
        compiler_params=_params(1, vmem),
        name="cross_attend_sample",
    )(z3, mem_k, mem_v)


def _silu(v):
    return v * (1.0 / (1.0 + jnp.exp(-v)))


def _gate_out_kernel(g0_ref, g1_ref, g2_ref, mix_ref, cross_ref, w_ref, x_ref, fg_ref, o_ref, *, final):
    half = MIX_W // 2
    u0 = (mix_ref[:, :half] * _silu(g0_ref[...])).astype(BF16)
    u1 = (mix_ref[:, half:] * _silu(g1_ref[...])).astype(BF16)
    u2 = (cross_ref[...] * _silu(g2_ref[...])).astype(BF16)
    y = x_ref[...] + (_dot(u0, w_ref[:half]) + _dot(u1, w_ref[half:MIX_W]) + _dot(u2, w_ref[MIX_W:]))
    if final:
        y = y * _rms_scale(y) * fg_ref[...]
    o_ref[...] = y


def _gate_out(z, gate_col, mix, cross, w_out, x, final_g, *, final):
    t, d = x.shape
    tm = min(t, 512)
    assert t % tm == 0 and X_W == MIX_W // 2
    vmem = 2 * tm * 4 * (3 * X_W + MIX_W + X_W + 2 * d) + 2 * BRANCH_W * d * 2 + (8 << 20)
    gate_spec = lambda c: pl.BlockSpec((tm, X_W), lambda i: (i, gate_col + c))
    return pl.pallas_call(
        functools.partial(_gate_out_kernel, final=final),
        grid=(t // tm,),
        in_specs=[gate_spec(0), gate_spec(1), gate_spec(2),
                  pl.BlockSpec((tm, MIX_W), lambda i: (i, 0)),
                  pl.BlockSpec((tm, X_W), lambda i: (i, 0)),
                  pl.BlockSpec((BRANCH_W, d), lambda i: (0, 0)),
                  pl.BlockSpec((tm, d), lambda i: (i, 0)),
                  pl.BlockSpec((1, d), lambda i: (0, 0))],
        out_specs=pl.BlockSpec((tm, d), lambda i: (i, 0)),
        out_shape=jax.ShapeDtypeStruct((t, d), F32),
        compiler_params=_params(1, vmem),
        name="gate_out",
    )(z, z, z, mix, cross, w_out, x, final_g.reshape(1, d))


def _rel_bucket(dist):
    n = jnp.maximum(dist, 0)
    nf = jnp.maximum(n, 1).astype(F32)
    large = MAX_EXACT + (jnp.log(nf / MAX_EXACT) / math.log(REL_MAX_DIST / MAX_EXACT)
                         * (NUM_BUCKETS - MAX_EXACT)).astype(jnp.int32)
    large = jnp.minimum(large, NUM_BUCKETS - 1)
    return jnp.where(n < MAX_EXACT, n, large)


def _bias_lookup(cols, dist, valid):
    bucket = _rel_bucket(jnp.asarray(dist, jnp.int32))[None]
    expand = (slice(None),) + (None,) * dist.ndim
    vals = jnp.zeros((cols.shape[1],) + dist.shape, F32)
    for b in range(NUM_BUCKETS):
        vals = jnp.where(bucket == b, cols[b].astype(F32)[expand], vals)
    return jnp.where(jnp.asarray(valid)[None], vals, NEG)


def _a_prompt_bias_tiles(rel_bias):
    back = np.arange(A_STEPS)[:, None] - (np.arange(2 * A_STEPS)[None, :] - A_STEPS)
    band = (back >= 0) & (back <= A_STEPS)
    return jnp.stack([_bias_lookup(rel_bias[:, g * A_KV_HEADS:(g + 1) * A_KV_HEADS],
                                   d * np.clip(back, 0, A_STEPS), band)
                      for g, d in enumerate(A_DILATIONS)])


def _c_prompt_bias_tiles(rel_bias, s, t):
    nd = s // t
    tile = jnp.arange(nd + 1, dtype=jnp.int32)[:, None, None]
    dist = tile * t + jnp.arange(t, dtype=jnp.int32)[None, :, None] - jnp.arange(t, dtype=jnp.int32)[None, None, :]
    dist = jnp.where(tile == nd, -1, dist)
    return _bias_lookup(rel_bias[:, C_BIAS_OFF:C_BIAS_OFF + C_HEADS], jnp.maximum(dist, 0), dist >= 0)


def _a_prompt_kernel(q0_ref, q1_ref, q2_ref, kc_ref, kp_ref, vc_ref, vp_ref, bias_ref, o_ref,
                     kk_ref, vv_ref, og_ref, lg_ref):
    blk = A_BLOCK
    first_block = pl.program_id(2) == 0
    kk_ref[:blk] = kp_ref[0]
    kk_ref[blk:] = kc_ref[0]
    vv_ref[:blk] = vp_ref[0]
    vv_ref[blk:] = vc_ref[0]
    col = lax.broadcasted_iota(jnp.int32, (A_STEPS, 2 * A_STEPS), 1)
    scale = A_HD ** -0.5

    def unit(u, g, d, q_ref):
        span = A_STEPS * d
        sb = u // d
        r = u - sb * d
        q0 = sb * span + r
        k0 = blk + q0 - span
        if d == 1:
            qs, ks = pl.ds(q0, A_STEPS), pl.ds(k0, 2 * A_STEPS)
        else:
            qs, ks = pl.ds(q0, A_STEPS, stride=d), pl.ds(k0, 2 * A_STEPS, stride=d)
        s = _dot_nt(q_ref[0, qs, :].astype(BF16), kk_ref[ks, :].astype(BF16)) * scale + bias_ref[g, 0]
        n_dead = jnp.where(jnp.logical_and(first_block, sb == 0), A_STEPS, 0)
        s = jnp.where(col < n_dead, NEG, s)
        m = jnp.max(s, axis=-1, keepdims=True)
        p = jnp.exp(s - m)
        l = jnp.sum(p, axis=-1, keepdims=True)
        og_ref[g, qs, :] = _dot(p.astype(BF16), vv_ref[ks, :].astype(BF16)) / l
        lg_ref[g, qs, :] = jnp.broadcast_to(m + jnp.log(l), (A_STEPS, A_HD))

    def units(u, _):
        for g, (d, q_ref) in enumerate(zip(A_DILATIONS, (q0_ref, q1_ref, q2_ref))):
            unit(u, g, d, q_ref)
        return 0

    lax.fori_loop(0, blk // A_STEPS, units, 0, unroll=8)

    l0, l1, l2 = lg_ref[0], lg_ref[1], lg_ref[2]
    m = jnp.maximum(jnp.maximum(l0, l1), l2)
    e0, e1, e2 = jnp.exp(l0 - m), jnp.exp(l1 - m), jnp.exp(l2 - m)
    den = e0 + e1 + e2
    o_ref[0] = (og_ref[0] * (e0 / den) + og_ref[1] * (e1 / den) + og_ref[2] * (e2 / den))


def _a_prompt(z3, bias_tiles):
    b, s, _ = z3.shape
    blk = A_BLOCK
    assert s % blk == 0
    h = A_KV_HEADS
    qspec = lambda g: pl.BlockSpec((1, blk, A_HD), lambda bi, hi, n: (bi, n, g * h + hi))
    kcol, vcol = A_Q // A_HD, (A_Q + A_KV) // A_HD
    cur = lambda c: pl.BlockSpec((1, blk, A_HD), lambda bi, hi, n: (bi, n, c + hi))
    prev = lambda c: pl.BlockSpec((1, blk, A_HD), lambda bi, hi, n: (bi, jnp.maximum(n - 1, 0), c + hi))
    vmem = 2 * 8 * blk * A_HD * 4 + 2 * 3 * A_STEPS * 2 * A_STEPS * 4 + (2 * 2 + 6) * blk * A_HD * 4 + (8 << 20)
    return pl.pallas_call(
        _a_prompt_kernel,
        grid=(b, h, s // blk),
        in_specs=[qspec(0), qspec(1), qspec(2), cur(kcol), prev(kcol), cur(vcol), prev(vcol),
                  pl.BlockSpec((len(A_DILATIONS), 1, A_STEPS, 2 * A_STEPS), lambda bi, hi, n: (0, hi, 0, 0))],
        out_specs=pl.BlockSpec((1, blk, A_HD), lambda bi, hi, n: (bi, n, hi)),
        out_shape=jax.ShapeDtypeStruct((b, s, A_KV), F32),
        scratch_shapes=[pltpu.VMEM((2 * blk, A_HD), F32), pltpu.VMEM((2 * blk, A_HD), F32),
                        pltpu.VMEM((len(A_DILATIONS), blk, A_HD), F32),
                        pltpu.VMEM((len(A_DILATIONS), blk, A_HD), F32)],
        compiler_params=_params(3, vmem),
        name="a_prompt",
    )(z3, z3, z3, z3, z3, z3, z3, bias_tiles)


def _later_matrix(t):
    return jnp.asarray(np.tril(np.ones((t, t), np.float32), -1), BF16)


def _stick_tile(z, later_mat, carry, valid):
    t = later_mat.shape[0]
    sp = _softplus2(z)
    drop = sp if valid is None else jnp.where(valid, sp, 0.0)
    later = []
    for c in reversed(range(z.shape[1] // t)):
        hi, lo = _split2(drop[:, c * t:(c + 1) * t])
        later.append(_dot(hi, later_mat) + _dot(lo, later_mat) + carry)
        carry = carry + jnp.sum(drop[:, c * t:(c + 1) * t], axis=-1, keepdims=True)
    a = jnp.exp2(z - sp - jnp.concatenate(later[::-1], axis=1))
    if valid is not None:
        a = jnp.where(valid, a, 0.0)
    return a, carry


B_QUERY_ROWS = 512
B_KEY_RUN = 512


def _b_prompt_kernel(q_ref, k_ref, v_ref, t_ref, o_ref, *, tq, tk):
    qi = pl.program_id(2)
    q = q_ref[0] * (B_HD ** -0.5 * LOG2_E)
    lane = lax.broadcasted_iota(jnp.int32, (tq, 2 * B_HD), 1)
    q2 = jnp.concatenate([jnp.where(lane < B_HD, q, 0.0), jnp.where(lane >= B_HD, q, 0.0)], axis=0).astype(BF16)
    later_mat = t_ref[...]

    def tile(kb, carry, acc, valid):
        start = pl.multiple_of(kb * tk, tk)
        k = k_ref[0, pl.ds(start, tk), :].astype(BF16)
        v = v_ref[0, pl.ds(start, tk), :].astype(BF16)
        a, carry = _stick_tile(_dot_nt(q2, k), later_mat, carry, valid)
        return carry, acc + _dot(a.astype(BF16), v)

    top = (qi * tq) // tk
    row = lax.broadcasted_iota(jnp.int32, (2 * tq, tk), 0)
    col = lax.broadcasted_iota(jnp.int32, (2 * tq, tk), 1)
    strictly_earlier = col + top * tk < jnp.where(row >= tq, row - tq, row) + qi * tq
    carry, acc = tile(top, jnp.zeros((2 * tq, 1), F32), jnp.zeros((2 * tq, 2 * B_HD), F32), strictly_earlier)
    carry, acc = lax.fori_loop(0, top, lambda it, c: tile(top - 1 - it, c[0], c[1], None), (carry, acc))
    o_ref[0] = jnp.where(lane < B_HD, acc[:tq], acc[tq:])


def _b_prompt(z3):
    b, s, _ = z3.shape
    tk = min(s, B_KEY_RUN)
    tq = min(tk, B_QUERY_ROWS)
    assert s % tk == 0 and tk % tq == 0 and tk % 2 == 0
    pairs = B_W // LANES
    vmem = 2 * (2 * s * LANES * 4 + 2 * tq * LANES * 4) + tk * tk + 10 * 2 * tq * tk * 4 + (4 << 20)
    return pl.pallas_call(
        functools.partial(_b_prompt_kernel, tq=tq, tk=tk),
        grid=(b, pairs, s // tq),
        in_specs=[pl.BlockSpec((1, tq, LANES), lambda bi, hp, qi: (bi, qi, hp)),
                  pl.BlockSpec((1, s, LANES), lambda bi, hp, qi: (bi, 0, pairs + hp)),
                  pl.BlockSpec((1, s, LANES), lambda bi, hp, qi: (bi, 0, 2 * pairs + hp)),
                  pl.BlockSpec((tk // 2, tk // 2), lambda bi, hp, qi: (0, 0))],
        out_specs=pl.BlockSpec((1, tq, LANES), lambda bi, hp, qi: (bi, qi, hp)),
        out_shape=jax.ShapeDtypeStruct((b, s, B_W), F32),
        compiler_params=_params(3, vmem),
        name="b_prompt",
    )(z3, z3, z3, _later_matrix(tk // 2))


def _diff_lambda(lam_ref, lam0):
    t = lam_ref[...]
    a = jnp.sum(t[0:1] * t[1:2], axis=-1, keepdims=True)
    b = jnp.sum(t[2:3] * t[3:4], axis=-1, keepdims=True)
    return jnp.exp(a) - jnp.exp(b) + lam0


def _online_softmax_step(s, v, m, l, acc):
    m_new = jnp.maximum(m, jnp.max(s, axis=-1, keepdims=True))
    alpha = jnp.exp2(m - m_new)
    p = jnp.exp2(s - m_new)
    return m_new, alpha * l + jnp.sum(p, axis=-1, keepdims=True), alpha * acc + _dot(p.astype(BF16), v)


def _c_prompt_kernel(q_ref, k_ref, v_ref, bias_ref, lam_ref, g_ref, o_ref, *, tq, lam0):
    qi = pl.program_id(2)
    n_tiles = bias_ref.shape[1]
    q = q_ref[0] * (C_HD ** -0.5 * LOG2_E)
    lane = lax.broadcasted_iota(jnp.int32, (tq, C_VD), 1)
    q2 = jnp.concatenate([jnp.where(lane < C_HD, q, 0.0), jnp.where(lane >= C_HD, q, 0.0)], axis=0).astype(BF16)

    def body(kb2, c):
        start = pl.multiple_of(kb2 * 2 * tq, 2 * tq)
        k = k_ref[0, pl.ds(start, 2 * tq), :].astype(BF16)
        v = v_ref[0, pl.ds(start, 2 * tq), :].astype(BF16)
        nd0 = qi - 2 * kb2
        nd1 = jnp.where(nd0 == 0, n_tiles - 1, nd0 - 1)
        bias = jnp.concatenate([bias_ref[0, nd0], bias_ref[0, nd1]], axis=1)
        s = _dot_nt(q2, k) + jnp.concatenate([bias, bias], axis=0)
        return _online_softmax_step(s, v, *c)

    col0 = lambda fill: jnp.full((2 * tq, 1), fill, F32)
    m, l, acc = lax.fori_loop(0, qi // 2 + 1, body, (col0(NEG), col0(0.0), jnp.zeros((2 * tq, C_VD), F32)))
    norm = acc / l
    o = norm[:tq] - _diff_lambda(lam_ref, lam0) * norm[tq:]
    o_ref[0] = o * _rms_scale(o) * g_ref[...] * (1.0 - lam0)


def _c_prompt(z3, bias_tiles, c_lam, subln_g, lam0):
    b, s, _ = z3.shape
    tq = bias_tiles.shape[-1]
    nd = s // tq
    assert s % (2 * tq) == 0 and bias_tiles.shape[1] == nd + 1
    h = C_HEADS
    vmem = 2 * (2 * s * C_VD * 4 + 2 * tq * C_VD * 4 + (nd + 1) * tq * tq * 4) + (16 << 20)
    return pl.pallas_call(
        functools.partial(_c_prompt_kernel, tq=tq, lam0=lam0),
        grid=(b, h, nd),
        in_specs=[pl.BlockSpec((1, tq, C_VD), lambda bi, hi, qi: (bi, qi, hi)),
                  pl.BlockSpec((1, s, C_VD), lambda bi, hi, qi: (bi, 0, h + hi)),
                  pl.BlockSpec((1, s, C_VD), lambda bi, hi, qi: (bi, 0, 2 * h + hi)),
                  pl.BlockSpec((1, nd + 1, tq, tq), lambda bi, hi, qi: (hi, 0, 0, 0)),
                  pl.BlockSpec((4, C_HD), lambda bi, hi, qi: (0, 0)),
                  pl.BlockSpec((1, C_VD), lambda bi, hi, qi: (0, 0))],
        out_specs=pl.BlockSpec((1, tq, C_VD), lambda bi, hi, qi: (bi, qi, hi)),
        out_shape=jax.ShapeDtypeStruct((b, s, C_W), F32),
        compiler_params=_params(3, vmem),
        name="c_prompt",
    )(z3, z3, z3, bias_tiles, c_lam, subln_g.reshape(1, C_VD))


A_TAIL = 512
A_PICK = 4


def _a_sample_slots(n_buf):
    period = A_DILATIONS[-1]
    tail = np.arange(n_buf - A_TAIL, n_buf)
    picks = (np.arange((n_buf - A_TAIL) // period)[:, None] * period + np.arange(A_PICK)[None, :]).reshape(-1)
    new = np.where(np.arange(LANES) < ROWS_S, n_buf + np.arange(LANES), -1)
    return np.concatenate([tail, picks, new])


def _a_sample_bias(rel_bias, n_buf, ds):
    pos = _a_sample_slots(n_buf)
    i = np.arange(ROWS_S)[:, None]
    delta = n_buf + i - pos[None, :]
    groups = []
    for g, d in enumerate(A_DILATIONS):
        ok = (pos[None] >= 0) & (delta >= 0) & (delta % d == 0) & (delta // d <= A_STEPS) & (i < ds)
        ok = ok | ((i >= ds) & (pos[None] == n_buf + i))
        groups.append(_bias_lookup(rel_bias[:, g * A_KV_HEADS:(g + 1) * A_KV_HEADS],
                                   np.clip(delta, 0, d * A_STEPS), ok))
    return jnp.concatenate(groups, axis=1)


def _a_sample_kernel(q_ref, kn_ref, vn_ref, kt_ref, ks_ref, vt_ref, vs_ref, bias_ref, o_ref,
                     knp_ref, vnp_ref, *, n_tail, n_pick):
    scale = A_HD ** -0.5
    n_groups = len(A_DILATIONS)
    knp_ref[...] = jnp.zeros(knp_ref.shape, F32)
    vnp_ref[...] = jnp.zeros(vnp_ref.shape, F32)
    tail_rows = A_TAIL // n_tail
    pick_m = LANES // A_PICK
    for h in range(A_KV_HEADS):
        sl = slice(h * A_HD, (h + 1) * A_HD)
        q = jnp.concatenate([q_ref[0, :, (g * A_KV_HEADS + h) * A_HD:(g * A_KV_HEADS + h + 1) * A_HD]
                             for g in range(n_groups)], axis=0).astype(BF16)
        knp_ref[:ROWS_S] = kn_ref[0, :, sl]
        vnp_ref[:ROWS_S] = vn_ref[0, :, sl]

        def slab(ref, i):
            if i < n_tail:
                m0 = i * tail_rows // 16
                x = ref[0][0, 0, m0:m0 + tail_rows // 16, pl.ds(h, 16, stride=SUBLANES), :]
            elif i < n_tail + n_pick:
                m0 = (i - n_tail) * pick_m
                x = ref[1][0, 0, m0:m0 + pick_m, pl.ds(h, A_PICK, stride=SUBLANES), :]
            else:
                return ref[2][...].astype(BF16)
            return x.reshape(LANES, A_HD).astype(BF16)

        n_slabs = n_tail + n_pick + 1
        krefs, vrefs = (kt_ref, ks_ref, knp_ref), (vt_ref, vs_ref, vnp_ref)
        s = jnp.concatenate([_dot_nt(q, slab(krefs, i)) for i in range(n_slabs)], axis=-1)
        s = s * scale + bias_ref[h]
        m = jnp.max(s, axis=-1, keepdims=True)
        p = jnp.exp(s - m)
        l = jnp.sum(p, axis=-1, keepdims=True)
        pb = p.astype(BF16)
        o = _dot(pb[:, :LANES], slab(vrefs, 0))
        for i in range(1, n_slabs):
            o = o + _dot(pb[:, i * LANES:(i + 1) * LANES], slab(vrefs, i))
        o = o / l
        lse = m + jnp.log(l)
        rows = lambda x, g: x[g * ROWS_S:(g + 1) * ROWS_S]
        mx = jnp.maximum(jnp.maximum(rows(lse, 0), rows(lse, 1)), rows(lse, 2))
        e = [jnp.exp(rows(lse, g) - mx) for g in range(n_groups)]
        den = e[0] + e[1] + e[2]
        o_ref[0, :, sl] = sum(rows(o, g) * (e[g] / den) for g in range(n_groups))


def _a_sample(zs3, cache_k, cache_v, j, bias):
    bs = zs3.shape[0]
    n_buf = cache_k.shape[2]
    period = A_DILATIONS[-1]
    assert n_buf % LANES == 0 and n_buf >= A_TAIL and (n_buf - A_TAIL) % (period * LANES // A_PICK) == 0
    n_per = n_buf // period
    tail_per = A_TAIL // period
    n_tail = A_TAIL // LANES
    n_pick = (n_per - tail_per) * A_PICK // LANES
    rows_per = period * A_KV_HEADS
    view = lambda c: c.reshape(c.shape[0], bs, n_per, rows_per, A_HD)
    tail = pl.BlockSpec((1, 1, tail_per, rows_per, A_HD), lambda b: (j, b, n_per // tail_per - 1, 0, 0))
    pick = pl.BlockSpec((1, 1, n_per - tail_per, A_PICK * A_KV_HEADS, A_HD), lambda b: (j, b, 0, 0, 0))
    n_slots = (n_tail + n_pick + 1) * LANES
    n_rows = len(A_DILATIONS) * ROWS_S
    vmem = 2 * 2 * (A_TAIL + (n_per - tail_per) * A_PICK) * A_KV_HEADS * A_HD * 4 + (12 << 20)
    kcol, vcol = A_Q // A_KV, A_Q // A_KV + 1
    return pl.pallas_call(
        functools.partial(_a_sample_kernel, n_tail=n_tail, n_pick=n_pick),
        grid=(bs,),
        in_specs=[pl.BlockSpec((1, ROWS_S, A_Q), lambda b: (b, 0, 0)),
                  pl.BlockSpec((1, ROWS_S, A_KV), lambda b: (b, 0, kcol)),
                  pl.BlockSpec((1, ROWS_S, A_KV), lambda b: (b, 0, vcol)),
                  tail, pick, tail, pick,
                  pl.BlockSpec((A_KV_HEADS, n_rows, n_slots), lambda b: (0, 0, 0))],
        out_specs=pl.BlockSpec((1, ROWS_S, A_KV), lambda b: (b, 0, 0)),
        out_shape=jax.ShapeDtypeStruct((bs, ROWS_S, A_KV), F32),
        scratch_shapes=[pltpu.VMEM((LANES, A_HD), F32), pltpu.VMEM((LANES, A_HD), F32)],
        compiler_params=_params(1, vmem),
        name="a_sample",
    )(zs3, zs3, zs3, view(cache_k), view(cache_k), view(cache_v), view(cache_v), bias)


PAGES_PER_STEP = 16


def _b_sample_kernel(pt_ref, q_ref, kn_ref, vn_ref, *refs, page, n_sub):
    del pt_ref
    kp_refs, vp_refs = refs[:n_sub], refs[n_sub:2 * n_sub]
    t_ref, o_ref, qbd_ref, knp_ref, vnp_ref, carry_ref, acc_ref = refs[2 * n_sub:]
    step = pl.program_id(1)
    later_mat = t_ref[...]
    n_rows = B_HEADS * ROWS_S
    own_head = (lax.broadcasted_iota(jnp.int32, (n_rows, B_W), 0) // ROWS_S
                == lax.broadcasted_iota(jnp.int32, (n_rows, B_W), 1) // B_HD)

    def attend(z, v_nt, valid):
        a, carry = _stick_tile(z, later_mat, carry_ref[...], valid)
        carry_ref[...] = carry
        ab = a.astype(BF16)
        acc_ref[...] += _dot(ab, vnp_ref[...].astype(BF16)) if v_nt is None else _dot_nt(ab, v_nt)

    @pl.when(step == 0)
    def _():
        carry_ref[...] = jnp.zeros(carry_ref.shape, F32)
        acc_ref[...] = jnp.zeros(acc_ref.shape, F32)
        knp_ref[...] = jnp.zeros(knp_ref.shape, F32)
        vnp_ref[...] = jnp.zeros(vnp_ref.shape, F32)
        knp_ref[:ROWS_S] = kn_ref[0]
        vnp_ref[:ROWS_S] = vn_ref[0]
        q = q_ref[0] * (B_HD ** -0.5 * LOG2_E)
        qbd = jnp.where(own_head, jnp.concatenate([q] * B_HEADS, axis=0), 0.0).astype(BF16)
        qbd_ref[...] = qbd
        i8 = lax.broadcasted_iota(jnp.int32, (n_rows, page), 0) % ROWS_S
        jj = lax.broadcasted_iota(jnp.int32, (n_rows, page), 1)
        attend(_dot_nt(qbd, knp_ref[...].astype(BF16)), None, jj < i8)

    nt = lambda page_refs: jnp.concatenate([r[0, 0].reshape(B_W, page).astype(BF16) for r in page_refs], axis=1)
    attend(_dot(qbd_ref[...], nt(kp_refs)), nt(vp_refs), None)

    @pl.when(step == pl.num_programs(1) - 1)
    def _():
        own = jnp.where(own_head, acc_ref[...], 0.0).reshape(B_HEADS, ROWS_S, B_W)
        o_ref[0] = jnp.sum(own, axis=0)


def _b_sample(zs3, cache_k, cache_v, j, page_table):
    bs = zs3.shape[0]
    n_pages = page_table.shape[1]
    page = cache_k.shape[2]
    n_sub = PAGES_PER_STEP
    assert page == LANES and n_pages % n_sub == 0
    view = lambda c: jnp.transpose(c, (0, 1, 3, 4, 2))
    page_spec = lambda u: pl.BlockSpec(
        (1, 1, B_HEADS, B_HD, page), lambda b, s, pt: (j, pt[b, n_pages - n_sub * (s + 1) + u], 0, 0, 0))
    kcol, vcol = 1, 2
    n_rows = B_HEADS * ROWS_S
    vmem = 2 * 2 * n_sub * B_W * page * 4 + 4 * page * B_W * 4 + (12 << 20)
    grid_spec = pltpu.PrefetchScalarGridSpec(
        num_scalar_prefetch=1,
        grid=(bs, n_pages // n_sub),
        in_specs=[pl.BlockSpec((1, ROWS_S, B_W), lambda b, s, pt: (b, 0, 0)),
                  pl.BlockSpec((1, ROWS_S, B_W), lambda b, s, pt: (b, 0, kcol)),
                  pl.BlockSpec((1, ROWS_S, B_W), lambda b, s, pt: (b, 0, vcol))]
                 + [page_spec(u) for u in range(n_sub)] * 2
                 + [pl.BlockSpec((page, page), lambda b, s, pt: (0, 0))],
        out_specs=pl.BlockSpec((1, ROWS_S, B_W), lambda b, s, pt: (b, 0, 0)),
        scratch_shapes=[pltpu.VMEM((n_rows, B_W), BF16),
                        pltpu.VMEM((page, B_W), F32), pltpu.VMEM((page, B_W), F32),
                        pltpu.VMEM((n_rows, 1), F32), pltpu.VMEM((n_rows, B_W), F32)])
    kt, vt = view(cache_k), view(cache_v)
    return pl.pallas_call(
        functools.partial(_b_sample_kernel, page=page, n_sub=n_sub),
        grid_spec=grid_spec,
        out_shape=jax.ShapeDtypeStruct((bs, ROWS_S, B_W), F32),
        compiler_params=_params(2, vmem),
        name="b_sample",
    )(page_table, zs3, zs3, zs3, *([kt] * n_sub), *([vt] * n_sub), _later_matrix(page))


def _c_sample_bias(rel_bias, n_pages, page, ds):
    past = n_pages * page
    i = np.arange(ROWS_S)
    dist_past = past + i[None, :, None] - (np.arange(n_pages)[:, None, None] * page + np.arange(page)[None, None, :])
    dist_new = i[:, None] - np.arange(page)[None, :]
    ok_new = (dist_new >= 0) & (np.arange(page)[None, :] < max(ds, 1))
    dist = np.concatenate([dist_past, np.maximum(dist_new, 0)[None]])
    ok = np.concatenate([np.ones_like(dist_past, bool), ok_new[None]])
    tiles = _bias_lookup(rel_bias[:, C_BIAS_OFF:C_BIAS_OFF + C_HEADS], dist, ok)
    tiles = jnp.broadcast_to(tiles[:, :, None], (C_HEADS, n_pages + 1, 2, ROWS_S, page))
    return jnp.moveaxis(tiles, 1, 0).reshape(n_pages + 1, C_HEADS * 2 * ROWS_S, page)


def _c_sample_kernel(pt_ref, q_ref, kn_ref, vn_ref, *refs, page, lam0, n_sub):
    del pt_ref
    kp_refs, vp_refs = refs[:n_sub], refs[n_sub:2 * n_sub]
    bnew_ref, bias_ref, lam_ref, g_ref, o_ref, knp_ref, vnp_ref, m_ref, l_ref, acc_ref = refs[2 * n_sub:]
    step = pl.program_id(1)
    rows_h = 2 * ROWS_S
    lane = lax.broadcasted_iota(jnp.int32, (ROWS_S, C_VD), 1)

    def q_head(h):
        q = q_ref[0, :, h * C_VD:(h + 1) * C_VD] * (C_HD ** -0.5)
        return jnp.concatenate([jnp.where(lane < C_HD, q, 0.0), jnp.where(lane >= C_HD, q, 0.0)],
                               axis=0).astype(BF16)

    def paged(page_refs):
        return lambda h: jnp.concatenate(
            [r[0, 0, pl.ds(h, page, stride=C_HEADS), :] for r in page_refs], axis=0).astype(BF16)

    def fresh(ref):
        return lambda h: ref[:, h * C_VD:(h + 1) * C_VD].astype(BF16)

    def attend(keys_of, vals_of, bias):
        s = jnp.concatenate([_dot_nt(q_head(h), keys_of(h)) for h in range(C_HEADS)], axis=0) + bias
        m_old = m_ref[...]
        m_new = jnp.maximum(m_old, jnp.max(s, axis=-1, keepdims=True))
        alpha = jnp.exp(m_old - m_new)
        p = jnp.exp(s - m_new)
        m_ref[...] = m_new
        l_ref[...] = alpha * l_ref[...] + jnp.sum(p, axis=-1, keepdims=True)
        pb = p.astype(BF16)
        for h in range(C_HEADS):
            rs = slice(h * rows_h, (h + 1) * rows_h)
            acc_ref[rs, :] = alpha[rs] * acc_ref[rs, :] + _dot(pb[rs], vals_of(h))

    @pl.when(step == 0)
    def _():
        m_ref[...] = jnp.full(m_ref.shape, NEG, F32)
        l_ref[...] = jnp.zeros(l_ref.shape, F32)
        acc_ref[...] = jnp.zeros(acc_ref.shape, F32)
        knp_ref[...] = jnp.zeros(knp_ref.shape, F32)
        vnp_ref[...] = jnp.zeros(vnp_ref.shape, F32)
        knp_ref[:ROWS_S] = kn_ref[0]
        vnp_ref[:ROWS_S] = vn_ref[0]
        attend(fresh(knp_ref), fresh(vnp_ref), bnew_ref[0])

    attend(paged(kp_refs), paged(vp_refs), jnp.concatenate([bias_ref[u] for u in range(n_sub)], axis=1))

    @pl.when(step == pl.num_programs(1) - 1)
    def _():
        lam = _diff_lambda(lam_ref, lam0)
        norm = acc_ref[...] / l_ref[...]
        outs = []
        for h in range(C_HEADS):
            o = norm[h * rows_h:h * rows_h + ROWS_S] - lam * norm[h * rows_h + ROWS_S:(h + 1) * rows_h]
            outs.append(o * _rms_scale(o) * g_ref[...] * (1.0 - lam0))
        o_ref[0] = jnp.concatenate(outs, axis=-1)


def _c_sample(zs3, cache_k, cache_v, j, page_table, bias, c_lam, subln_g, lam0):
    bs = zs3.shape[0]
    n_pages = page_table.shape[1]
    page = cache_k.shape[2]
    n_sub = PAGES_PER_STEP
    assert page == LANES and n_pages % n_sub == 0
    rows = page * C_HEADS
    n_rows = C_HEADS * 2 * ROWS_S
    view = lambda c: c.reshape(c.shape[0], c.shape[1], rows, C_VD)
    page_spec = lambda u: pl.BlockSpec((1, 1, rows, C_VD), lambda b, s, pt: (j, pt[b, n_sub * s + u], 0, 0))
    kcol, vcol = 1, 2
    vmem = 2 * 2 * n_sub * rows * C_VD * 4 + 2 * page * C_W * 4 + (12 << 20)
    grid_spec = pltpu.PrefetchScalarGridSpec(
        num_scalar_prefetch=1,
        grid=(bs, n_pages // n_sub),
        in_specs=[pl.BlockSpec((1, ROWS_S, C_W), lambda b, s, pt: (b, 0, 0)),
                  pl.BlockSpec((1, ROWS_S, C_W), lambda b, s, pt: (b, 0, kcol)),
                  pl.BlockSpec((1, ROWS_S, C_W), lambda b, s, pt: (b, 0, vcol))]
                 + [page_spec(u) for u in range(n_sub)] * 2
                 + [pl.BlockSpec((1, n_rows, page), lambda b, s, pt: (n_pages, 0, 0)),
                    pl.BlockSpec((n_sub, n_rows, page), lambda b, s, pt: (s, 0, 0)),
                    pl.BlockSpec((4, C_HD), lambda b, s, pt: (0, 0)),
                    pl.BlockSpec((1, C_VD), lambda b, s, pt: (0, 0))],
        out_specs=pl.BlockSpec((1, ROWS_S, C_W), lambda b, s, pt: (b, 0, 0)),
        scratch_shapes=[pltpu.VMEM((page, C_W), F32), pltpu.VMEM((page, C_W), F32),
                        pltpu.VMEM((n_rows, 1), F32), pltpu.VMEM((n_rows, 1), F32),
                        pltpu.VMEM((n_rows, C_VD), F32)])
    kc, vc = view(cache_k), view(cache_v)
    return pl.pallas_call(
        functools.partial(_c_sample_kernel, page=page, lam0=lam0, n_sub=n_sub),
        grid_spec=grid_spec,
        out_shape=jax.ShapeDtypeStruct((bs, ROWS_S, C_W), F32),
        compiler_params=_params(2, vmem),
        name="c_sample",
    )(page_table, zs3, zs3, zs3, *([kc] * n_sub), *([vc] * n_sub), bias, bias, c_lam, subln_g.reshape(1, C_VD))


def _lambda_init(layer):
    return 0.8 - 0.6 * math.exp(-0.3 * layer)


def kernel(x_prompt, x_sample, cache_a_k, cache_a_v, cache_b_k, cache_b_v, cache_c_k, cache_c_v,
           cache_mem_k, cache_mem_v, page_table, mem_prompt, norm_g, final_g, mem_norm_g, w_mem_kv,
           rel_bias, w_in_a, w_in_b, w_in_c, c_lambda, c_subln_g, w_out):
    bp, s, d = x_prompt.shape
    bs, ds, _ = x_sample.shape
    depth = norm_g.shape[0]
    n_pages, page = page_table.shape[1], cache_b_k.shape[2]
    n_buf = cache_a_k.shape[2]
    win_p = min(n_buf, s)
    assert ds <= ROWS_S

    a_tiles = _a_prompt_bias_tiles(rel_bias)
    a_sample_bias = _a_sample_bias(rel_bias, n_buf, ds)
    c_tiles = _c_prompt_bias_tiles(rel_bias, s, min(s // 2, 256)) * LOG2_E
    c_sample_bias = _c_sample_bias(rel_bias, n_pages, page, ds)
    mem_k_all = cache_mem_k.reshape(depth, bs, MEM_LEN * X_HEADS, X_HD)
    mem_v_all = cache_mem_v.reshape(depth, bs, MEM_LEN * X_HEADS, X_HD)

    xp = x_prompt.reshape(bp * s, d)
    xs = jnp.pad(x_sample, ((0, 0), (0, ROWS_S - ds), (0, 0))).reshape(bs * ROWS_S, d)
    mem = mem_prompt.reshape(bp * MEM_LEN, d)
    outs = {name: [] for name in ("a_k_p", "a_v_p", "b_k_p", "b_v_p", "c_k_p", "c_v_p", "mem_k_p", "mem_v_p",
                                  "a_k_s", "a_v_s", "b_k_s", "b_v_s", "c_k_s", "c_v_s")}

    for l in range(depth):
        m, j = l % N_MIXERS, l // N_MIXERS
        w_in = (w_in_a, w_in_b, w_in_c)[m][j].astype(BF16)
        n_in = w_in.shape[1]
        n_mix = n_in - X_W - BRANCH_W
        zp = _norm_matmul(xp, norm_g[l], w_in)
        zs = _norm_matmul(xs, norm_g[l], w_in)
        mkv = _norm_matmul(mem, mem_norm_g[l], w_mem_kv[l].astype(BF16)).reshape(bp, MEM_LEN, 2 * X_W)
        outs["mem_k_p"].append(mkv[..., :X_W].reshape(bp, MEM_LEN, X_HEADS, X_HD))
        outs["mem_v_p"].append(mkv[..., X_W:].reshape(bp, MEM_LEN, X_HEADS, X_HD))
        zp3 = zp.reshape(bp, s, n_in)
        zs3 = zs.reshape(bs, ROWS_S, n_in)

        if m == 0:
            op = _a_prompt(zp3, a_tiles)
            os_ = _a_sample(zs3, cache_a_k, cache_a_v, j, a_sample_bias)
            heads = (A_KV_HEADS, A_HD)
            kcols, vcols = slice(A_Q, A_Q + A_KV), slice(A_Q + A_KV, A_Q + 2 * A_KV)
            outs["a_k_p"].append(zp3[:, s - win_p:, kcols].reshape(bp, win_p, *heads))
            outs["a_v_p"].append(zp3[:, s - win_p:, vcols].reshape(bp, win_p, *heads))
            outs["a_k_s"].append(zs3[:, :ds, kcols].reshape(bs, ds, *heads))
            outs["a_v_s"].append(zs3[:, :ds, vcols].reshape(bs, ds, *heads))
        elif m == 1:
            heads = (B_HEADS, B_HD)
            kcols, vcols = slice(B_W, 2 * B_W), slice(2 * B_W, 3 * B_W)
            op = _b_prompt(zp3)
            os_ = _b_sample(zs3, cache_b_k, cache_b_v, j, page_table)
            outs["b_k_p"].append(zp3[:, :, kcols].reshape(bp, s, *heads))
            outs["b_v_p"].append(zp3[:, :, vcols].reshape(bp, s, *heads))
            outs["b_k_s"].append(zs3[:, :ds, kcols].reshape(bs, ds, *heads))
            outs["b_v_s"].append(zs3[:, :ds, vcols].reshape(bs, ds, *heads))
        else:
            heads = (C_HEADS, C_VD)
            kcols, vcols = slice(C_W, 2 * C_W), slice(2 * C_W, 3 * C_W)
            lam0 = _lambda_init(l)
            op = _c_prompt(zp3, c_tiles, c_lambda[j], c_subln_g[j], lam0)
            os_ = _c_sample(zs3, cache_c_k, cache_c_v, j, page_table, c_sample_bias,
                            c_lambda[j], c_subln_g[j], lam0)
            outs["c_k_p"].append(zp3[:, :, kcols].reshape(bp, s, *heads))
            outs["c_v_p"].append(zp3[:, :, vcols].reshape(bp, s, *heads))
            outs["c_k_s"].append(zs3[:, :ds, kcols].reshape(bs, ds, *heads))
            outs["c_v_s"].append(zs3[:, :ds, vcols].reshape(bs, ds, *heads))

        q_col = n_mix // X_W
        cross_p = _cross_attend(zp3, q_col, mkv[None], 0, mkv[None], 1, 0, nb=1, rows=min(s, 512))
        cross_s = _cross_attend_sample(zs3, q_col, mem_k_all, mem_v_all, l, nb=8)
        last = l == depth - 1
        xp = _gate_out(zp, q_col + 1, op.reshape(bp * s, MIX_W), cross_p.reshape(bp * s, X_W),
                       w_out[l].astype(BF16), xp, final_g, final=last)
        xs = _gate_out(zs, q_col + 1, os_.reshape(bs * ROWS_S, MIX_W), cross_s.reshape(bs * ROWS_S, X_W),
                       w_out[l].astype(BF16), xs, final_g, final=last)

    y_prompt = xp.reshape(bp, s, d)
    y_sample = xs.reshape(bs, ROWS_S, d)[:, :ds]
    stack = lambda name: jnp.stack(outs[name])
    return (y_prompt, y_sample,
            stack("a_k_p"), stack("a_v_p"), stack("b_k_p"), stack("b_v_p"), stack("c_k_p"), stack("c_v_p"),
            stack("mem_k_p"), stack("mem_v_p"),
            stack("a_k_s"), stack("a_v_s"), stack("b_k_s"), stack("b_v_s"), stack("c_k_s"), stack("c_v_s"))
```
